```python
import functools
import jax, jax.numpy as jnp
from jax import lax
import numpy as np

D_MODEL = 1024
BATCH = 8
SEQ = 8192
DEPTH = 1
DEC_BATCH = 8
DEC_SEQ = 32
PAST_LEN = 4096

CHUNK = 64
LEFT_CHUNKS = 8
D_MIX = D_MODEL
D_ATTN = D_MIX // 2
D_POOL = D_MIX - D_ATTN
HEAD_DIM = 64
N_HEADS = D_ATTN // HEAD_DIM
ATTN_SCALE = HEAD_DIM ** -0.5
REL_CLIP = 128
N_REL = 2 * REL_CLIP + 1
POOL_WINDOWS = (2, 4, 8, 16)
N_POOL_GROUPS = len(POOL_WINDOWS)
POOL_GROUP_DIM = D_POOL // N_POOL_GROUPS
POOL_HIST = max(POOL_WINDOWS) - 1
D_IN = 3 * D_ATTN + D_POOL
D_FF = 4 * D_MODEL
D_PLE = 256
EPS = 1e-6
NEG_INF = -1e30

kernel_name = "hybrid_chunk_band_attn_pool_streaming_step"


def rmsnorm(x, g):
    xf = x.astype(jnp.float32)
    y = xf * lax.rsqrt(jnp.mean(xf * xf, axis=-1, keepdims=True) + EPS)
    return (y * g.astype(jnp.float32)).astype(x.dtype)


def band_attention(q, kb, vb, qpos, kpos, kvalid, rel_bias):
    s = jnp.einsum("bnqhd,bnkhd->bnhqk", q, kb).astype(jnp.float32) * ATTN_SCALE
    rel = jnp.clip(qpos[:, :, None] - kpos[:, None, :], -REL_CLIP, REL_CLIP) + REL_CLIP
    bias = jnp.moveaxis(jnp.take(rel_bias.astype(jnp.float32), rel, axis=1), 0, 1)
    s = jnp.where(kvalid[None, :, None, None, :], s + bias[None], NEG_INF)
    pr = jax.nn.softmax(s, axis=-1).astype(vb.dtype)
    return jnp.einsum("bnhqk,bnkhd->bnqhd", pr, vb)


def multiscale_pool(u_ext, n_hist, w_pool, pool_scale):
    B, L, C = u_ext.shape
    T = L - n_hist
    cs = jnp.concatenate([jnp.zeros((B, 1, C), jnp.float32),
                          jnp.cumsum(u_ext.astype(jnp.float32), axis=1)], axis=1)
    idx = n_hist + jnp.arange(T)
    outs = []
    for g, w in enumerate(POOL_WINDOWS):
        lo, hi = g * POOL_GROUP_DIM, (g + 1) * POOL_GROUP_DIM
        start = jnp.maximum(idx + 1 - w, 0)
        cnt = (idx + 1 - start).astype(jnp.float32)
        csg = cs[:, :, lo:hi]
        mean = (csg[:, idx + 1] - csg[:, start]) / cnt[None, :, None]
        diff = (mean - u_ext[:, n_hist:, lo:hi].astype(jnp.float32)).astype(u_ext.dtype)
        outs.append(jnp.einsum("btc,cd->btd", diff, w_pool[g]))
    return jnp.concatenate(outs, axis=-1) * pool_scale


def prompt_mixer(q, k, v, u, rel_bias, w_pool, pool_scale):
    B, T, H, Dh = q.shape
    nc = T // CHUNK
    left = LEFT_CHUNKS * CHUNK
    band = left + CHUNK
    pad = ((0, 0), (left, 0), (0, 0), (0, 0))
    kp = jnp.pad(k, pad)
    vp = jnp.pad(v, pad)
    idx = (jnp.arange(nc) * CHUNK)[:, None] + jnp.arange(band)[None, :]
    kpos = idx - left
    qpos = jnp.arange(T).reshape(nc, CHUNK)
    attn = band_attention(q.reshape(B, nc, CHUNK, H, Dh), kp[:, idx], vp[:, idx],
                          qpos, kpos, kpos >= 0, rel_bias).reshape(B, T, H * Dh)
    pool = multiscale_pool(u, 0, w_pool, pool_scale)
    keep = min(left, T)
    return attn, pool, (k[:, T - keep:], v[:, T - keep:], u[:, T - POOL_HIST:])


def sample_mixer(q, k, v, u, cache_k, cache_v, state_pool, rel_bias, w_pool, pool_scale):
    B, T, H, Dh = q.shape
    win = cache_k.shape[1]
    kb = jnp.concatenate([cache_k.astype(k.dtype), k], axis=1)[:, None]
    vb = jnp.concatenate([cache_v.astype(v.dtype), v], axis=1)[:, None]
    qpos = (PAST_LEN + jnp.arange(T))[None]
    kpos = (PAST_LEN - win + jnp.arange(win + T))[None]
    attn = band_attention(q[:, None], kb, vb, qpos, kpos, jnp.ones(kpos.shape, bool),
                          rel_bias).reshape(B, T, H * Dh)
    u_ext = jnp.concatenate([state_pool.astype(u.dtype), u], axis=1)
    pool = multiscale_pool(u_ext, POOL_HIST, w_pool, pool_scale)
    return attn, pool, (k, v, u_ext[:, -POOL_HIST:])


def trunk_layer(x, p, mixer, g_mix_pre, g_mix_post, g_ff_pre, g_ff_post, g_ple_post,
                w_in, w_out, w_ff1, w_ff2, w_ple, w_ple_gate):
    B, T, _ = x.shape
    h = rmsnorm(x, g_mix_pre)
    qkvu = jnp.einsum("btd,de->bte", h, w_in)
    q = qkvu[..., :D_ATTN].reshape(B, T, N_HEADS, HEAD_DIM)
    k = qkvu[..., D_ATTN:2 * D_ATTN].reshape(B, T, N_HEADS, HEAD_DIM)
    v = qkvu[..., 2 * D_ATTN:3 * D_ATTN].reshape(B, T, N_HEADS, HEAD_DIM)
    u = qkvu[..., 3 * D_ATTN:]
    attn, pool, state = mixer(q, k, v, u)
    mix = jnp.einsum("bte,ed->btd", jnp.concatenate([attn, pool], axis=-1), w_out)
    x = x + rmsnorm(mix, g_mix_post)
    hf = jnp.einsum("btd,df->btf", rmsnorm(x, g_ff_pre), w_ff1)
    ff = jnp.einsum("btf,fd->btd", jnp.square(jax.nn.relu(hf)), w_ff2)
    x = x + rmsnorm(ff, g_ff_post)
    gate = jax.nn.sigmoid(jnp.einsum("btd,de->bte", x, w_ple_gate))
    ple = gate * jnp.einsum("btp,pd->btd", p, w_ple)
    x = x + rmsnorm(ple, g_ple_post)
    return x, state


def setup_inputs(seed: int = 0) -> dict:
    key = jax.random.key(seed)
    ks = jax.random.split(key, 32)
    f32 = jnp.float32
    kv_win = min(LEFT_CHUNKS * CHUNK, PAST_LEN)

    def nrm(k, shape, scale):
        return scale * jax.random.normal(k, shape, f32)

    def gain(k, n):
        return 1.0 + 0.05 * jax.random.normal(k, (DEPTH, n), f32)

    return {
        "x_prompt": nrm(ks[0], (BATCH, SEQ, D_MODEL), 1.0),
        "x_sample": nrm(ks[1], (DEC_BATCH, DEC_SEQ, D_MODEL), 1.0),
        "cache_k": nrm(ks[2], (DEPTH, DEC_BATCH, kv_win, N_HEADS, HEAD_DIM), 1.0),
        "cache_v": nrm(ks[3], (DEPTH, DEC_BATCH, kv_win, N_HEADS, HEAD_DIM), 1.0),
        "state_pool": nrm(ks[4], (DEPTH, DEC_BATCH, POOL_HIST, D_POOL), 1.0),
        "p_prompt": nrm(ks[5], (DEPTH, BATCH, SEQ, D_PLE), 1.0),
        "p_sample": nrm(ks[6], (DEPTH, DEC_BATCH, DEC_SEQ, D_PLE), 1.0),
        "g_mix_pre": gain(ks[7], D_MODEL),
        "g_mix_post": gain(ks[8], D_MODEL),
        "g_ff_pre": gain(ks[9], D_MODEL),
        "g_ff_post": gain(ks[10], D_MODEL),
        "g_ple_post": gain(ks[11], D_MODEL),
        "w_in": nrm(ks[12], (DEPTH, D_MODEL, D_IN), D_MODEL ** -0.5),
        "rel_bias": nrm(ks[13], (DEPTH, N_HEADS, N_REL), 0.1),
        "w_pool": nrm(ks[14], (DEPTH, N_POOL_GROUPS, POOL_GROUP_DIM, POOL_GROUP_DIM), POOL_GROUP_DIM ** -0.5),
        "pool_scale": 1.0 + 0.1 * jax.random.normal(ks[15], (DEPTH, D_POOL), f32),
        "w_out": nrm(ks[16], (DEPTH, D_MIX, D_MODEL), D_MIX ** -0.5),
        "w_ff1": nrm(ks[17], (DEPTH, D_MODEL, D_FF), D_MODEL ** -0.5),
        "w_ff2": nrm(ks[18], (DEPTH, D_FF, D_MODEL), D_FF ** -0.5),
        "w_ple": nrm(ks[19], (DEPTH, D_PLE, D_MODEL), D_PLE ** -0.5),
        "w_ple_gate": nrm(ks[20], (DEPTH, D_MODEL, D_MODEL), D_MODEL ** -0.5),
    }


def reference(x_prompt, x_sample, cache_k, cache_v, state_pool, p_prompt, p_sample,
              g_mix_pre, g_mix_post, g_ff_pre, g_ff_post, g_ple_post,
              w_in, rel_bias, w_pool, pool_scale, w_out, w_ff1, w_ff2, w_ple, w_ple_gate):
    xp, xs = x_prompt, x_sample
    kp_l, vp_l, up_l, ks_l, vs_l, us_l = [], [], [], [], [], []
    for i in range(DEPTH):
        shared = (g_mix_pre[i], g_mix_post[i], g_ff_pre[i], g_ff_post[i], g_ple_post[i],
                  w_in[i], w_out[i], w_ff1[i], w_ff2[i], w_ple[i], w_ple_gate[i])
        mix_p = functools.partial(prompt_mixer, rel_bias=rel_bias[i], w_pool=w_pool[i],
                                  pool_scale=pool_scale[i])
        mix_s = functools.partial(sample_mixer, cache_k=cache_k[i], cache_v=cache_v[i],
                                  state_pool=state_pool[i], rel_bias=rel_bias[i],
                                  w_pool=w_pool[i], pool_scale=pool_scale[i])
        xp, (kp, vp, up) = trunk_layer(xp, p_prompt[i], mix_p, *shared)
        xs, (kn, vn, un) = trunk_layer(xs, p_sample[i], mix_s, *shared)
        kp_l.append(kp); vp_l.append(vp); up_l.append(up)
        ks_l.append(kn); vs_l.append(vn); us_l.append(un)
    k_prompt_new = jnp.stack(kp_l)
    v_prompt_new = jnp.stack(vp_l)
    pool_prompt_new = jnp.stack(up_l)
    k_sample_new = jnp.stack(ks_l)
    v_sample_new = jnp.stack(vs_l)
    pool_sample_new = jnp.stack(us_l)
    return (xp, xs, k_prompt_new, v_prompt_new, pool_prompt_new, k_sample_new, v_sample_new, pool_sample_new)
```

```python
import functools

import jax
import jax.numpy as jnp
from jax import lax
from jax.experimental import pallas as pl
from jax.experimental.pallas import tpu as pltpu

D_MODEL = 1024
D_ATTN = 512
D_POOL = 512
HEAD_DIM = 64
N_HEADS = 8
CHUNK = 64
LEFT_CHUNKS = 8
LEFT = LEFT_CHUNKS * CHUNK
REL_CLIP = 128
N_REL = 2 * REL_CLIP + 1
POOL_WINDOWS = (2, 4, 8, 16)
POOL_GROUP_DIM = 128
POOL_HIST = 15
HIST_ROWS = 16
D_FF = 4096
FF_CHUNK = 1024
D_PLE = 256
EPS = 1e-6
NEG_INF = -1e30
ATTN_SCALE = HEAD_DIM ** -0.5

TQ = 256
LK = LEFT + TQ
SAMPLE_TQ = 128
SAMPLE_LK = LEFT + SAMPLE_TQ
TABLE_PAD = 256
VMEM_LIMIT_BYTES = 56 * 1024 * 1024

F32 = jnp.float32
BF16 = jnp.bfloat16


def _bias_table_kernel(rbt_ref, out_ref, buf_a, buf_b):
    far = jnp.broadcast_to(rbt_ref[0, 0:1, :], (TABLE_PAD + LEFT - REL_CLIP, TQ))
    buf_a[0:TABLE_PAD + LEFT - REL_CLIP, :] = far
    buf_b[0:TABLE_PAD, :] = far[0:TABLE_PAD]
    buf_a[TABLE_PAD + LEFT - REL_CLIP:TABLE_PAD + LEFT + REL_CLIP, :] = jnp.broadcast_to(
        rbt_ref[0, 0:2 * REL_CLIP, :], (2 * REL_CLIP, TQ))
    buf_a[TABLE_PAD + LEFT + REL_CLIP:TABLE_PAD + LK, :] = jnp.broadcast_to(
        rbt_ref[0, 2 * REL_CLIP:2 * REL_CLIP + 1, :], (LK - LEFT - REL_CLIP, TQ))

    rows = 128
    qidx = lax.broadcasted_iota(jnp.int32, (rows, TQ), 1)
    src, dst = buf_a, buf_b
    for b in range(TQ.bit_length() - 1):
        s = 1 << b
        bit = (qidx & s) != 0
        for r0 in range(TABLE_PAD, TABLE_PAD + LK, rows):
            dst[r0:r0 + rows, :] = jnp.where(bit, src[r0 - s:r0 - s + rows, :], src[r0:r0 + rows, :])
        src, dst = dst, src

    kk = lax.broadcasted_iota(jnp.int32, (LK, TQ), 0)
    qq = lax.broadcasted_iota(jnp.int32, (LK, TQ), 1)
    d = (kk >> 6) - (qq >> 6)
    out_ref[0] = jnp.where((d >= 0) & (d <= LEFT_CHUNKS), src[TABLE_PAD:TABLE_PAD + LK, :], NEG_INF)


def _bias_table(rel_bias):
    rbt = jnp.pad(rel_bias[:, ::-1], ((0, 0), (0, 384 - N_REL)))[:, :, None]
    return pl.pallas_call(
        _bias_table_kernel,
        grid=(N_HEADS,),
        in_specs=[pl.BlockSpec((1, 384, 1), lambda h: (h, 0, 0))],
        out_specs=pl.BlockSpec((1, LK, TQ), lambda h: (h, 0, 0)),
        out_shape=jax.ShapeDtypeStruct((N_HEADS, LK, TQ), F32),
        scratch_shapes=[pltpu.VMEM((TABLE_PAD + LK, TQ), F32), pltpu.VMEM((TABLE_PAD + LK, TQ), F32)],
        name="bias_table",
    )(rbt)


def _rmsnorm(x, g):
    y = x * lax.rsqrt(jnp.mean(x * x, axis=-1, keepdims=True) + EPS)
    return y * g


def _dot(a, b):
    return jnp.dot(a, b, preferred_element_type=F32)


def _project(x, gains_ref, w_in_ref):
    h = _rmsnorm(x, gains_ref[0:1, :]).astype(BF16)
    qkvu = _dot(h, w_in_ref[...])
    q = qkvu[:, 0:D_ATTN] * ATTN_SCALE
    k = qkvu[:, D_ATTN:2 * D_ATTN]
    v = qkvu[:, 2 * D_ATTN:3 * D_ATTN]
    u = qkvu[:, 3 * D_ATTN:]
    return q, k, v, u


def _attention_t(q_t, kext_ref, vt_ref, table_ref, valid):
    nq = q_t.shape[1]
    row = lax.broadcasted_iota(jnp.int32, (2 * HEAD_DIM, nq), 0)
    outs = []
    for pair in range(N_HEADS // 2):
        kp = kext_ref[:, 2 * HEAD_DIM * pair:2 * HEAD_DIM * (pair + 1)]
        qp = q_t[2 * HEAD_DIM * pair:2 * HEAD_DIM * (pair + 1), :]
        for e in range(2):
            h = 2 * pair + e
            own = (row < HEAD_DIM) if e == 0 else (row >= HEAD_DIM)
            w = jnp.where(own, qp, 0.0).astype(BF16)
            s = _dot(kp, w)
            s = jnp.where(valid, s + table_ref[h], NEG_INF)
            m = jnp.max(s, axis=0, keepdims=True)
            p = jnp.exp(s - m)
            l = jnp.sum(p, axis=0, keepdims=True)
            o = _dot(vt_ref[HEAD_DIM * h:HEAD_DIM * (h + 1), :], p.astype(BF16))
            outs.append(o / l)
    return jnp.concatenate(outs, axis=0)


def _pool(uext_ref, n_rows, frames_before, w_pool_ref, pool_scale):
    rows = lax.broadcasted_iota(jnp.int32, (n_rows, POOL_GROUP_DIM), 0)
    outs = []
    for g, w in enumerate(POOL_WINDOWS):
        lo = g * POOL_GROUP_DIM
        cur = uext_ref[HIST_ROWS:HIST_ROWS + n_rows, lo:lo + POOL_GROUP_DIM]
        acc = cur
        for j in range(1, w):
            acc = acc + uext_ref[HIST_ROWS - j:HIST_ROWS - j + n_rows, lo:lo + POOL_GROUP_DIM]
        cnt = jnp.minimum(w, frames_before + rows + 1).astype(F32)
        diff = acc / cnt - cur
        outs.append(_dot(diff.astype(BF16), w_pool_ref[g]))
    return jnp.concatenate(outs, axis=-1) * pool_scale


def _finish(x, attn, pool, p, gains_ref, w_out_ref, w_ff1_ref, w_ff2_ref, w_ple_ref, w_gate_ref):
    mix = _dot(jnp.concatenate([attn, pool], axis=-1).astype(BF16), w_out_ref[...])
    x = x + _rmsnorm(mix, gains_ref[1:2, :])
    hn = _rmsnorm(x, gains_ref[2:3, :]).astype(BF16)
    ff = None
    for c in range(D_FF // FF_CHUNK):
        hf = _dot(hn, w_ff1_ref[:, c * FF_CHUNK:(c + 1) * FF_CHUNK])
        act = jnp.square(jnp.maximum(hf, 0.0)).astype(BF16)
        part = _dot(act, w_ff2_ref[c * FF_CHUNK:(c + 1) * FF_CHUNK, :])
        ff = part if ff is None else ff + part
    x = x + _rmsnorm(ff, gains_ref[3:4, :])
    gate = jax.nn.sigmoid(_dot(x.astype(BF16), w_gate_ref[...]))
    ple = gate * _dot(p.astype(BF16), w_ple_ref[...])
    return x + _rmsnorm(ple, gains_ref[4:5, :])


def _prompt_kernel(x_ref, p_ref, table_ref, gains_ref, pscale_ref, w_in_ref, w_pool_ref, w_out_ref,
                   w_ff1_ref, w_ff2_ref, w_ple_ref, w_gate_ref,
                   y_ref, knew_ref, vnew_ref, unew_ref, kext, vt, uext):
    t = pl.program_id(1)

    @pl.when(t == 0)
    def _():
        kext[...] = jnp.zeros_like(kext)
        vt[...] = jnp.zeros_like(vt)
        uext[0:HIST_ROWS, :] = jnp.zeros((HIST_ROWS, D_POOL), F32)

    x = x_ref[0]
    q, k, v, u = _project(x, gains_ref, w_in_ref)
    knew_ref[0] = k
    vnew_ref[0] = v
    unew_ref[0] = u[TQ - HIST_ROWS:, :]

    kext[0:LEFT, :] = kext[TQ:LK, :]
    kext[LEFT:LK, :] = k.astype(BF16)
    vt[:, 0:LEFT] = vt[:, TQ:LK]
    vt[:, LEFT:LK] = v.T.astype(BF16)
    uext[HIST_ROWS:HIST_ROWS + TQ, :] = u

    kk = lax.broadcasted_iota(jnp.int32, (LK, TQ), 0)
    valid = kk + (t * TQ - LEFT) >= 0
    attn = _attention_t(q.T, kext, vt, table_ref, valid).T
    pool = _pool(uext, TQ, t * TQ, w_pool_ref, pscale_ref[...])
    uext[0:HIST_ROWS, :] = u[TQ - HIST_ROWS:, :]

    y_ref[0] = _finish(x, attn, pool, p_ref[0], gains_ref, w_out_ref, w_ff1_ref, w_ff2_ref, w_ple_ref,
                       w_gate_ref)


def _whole(space=pltpu.VMEM):
    return pl.BlockSpec(memory_space=space)


def _prompt_layer(x, p, table, gains, pscale, weights):
    batch, seq, _ = x.shape
    assert seq % TQ == 0 and LEFT % TQ == 0 and seq >= LEFT
    n_tiles = seq // TQ
    keep_tiles = LEFT // TQ
    tile = lambda b, t: (b, t, 0)
    last = lambda b, t: (b, jnp.maximum(t - (n_tiles - keep_tiles), 0), 0)
    return pl.pallas_call(
        _prompt_kernel,
        grid=(batch, n_tiles),
        in_specs=[pl.BlockSpec((1, TQ, D_MODEL), tile), pl.BlockSpec((1, TQ, D_PLE), tile)]
        + [_whole()] * (3 + len(weights)),
        out_specs=[pl.BlockSpec((1, TQ, D_MODEL), tile), pl.BlockSpec((1, TQ, D_ATTN), last),
                   pl.BlockSpec((1, TQ, D_ATTN), last), pl.BlockSpec((1, HIST_ROWS, D_POOL), lambda b, t: (b, 0, 0))],
        out_shape=[jax.ShapeDtypeStruct((batch, seq, D_MODEL), F32),
                   jax.ShapeDtypeStruct((batch, LEFT, D_ATTN), F32),
                   jax.ShapeDtypeStruct((batch, LEFT, D_ATTN), F32),
                   jax.ShapeDtypeStruct((batch, HIST_ROWS, D_POOL), F32)],
        scratch_shapes=[pltpu.VMEM((LK, D_ATTN), BF16), pltpu.VMEM((D_ATTN, LK), BF16),
                        pltpu.VMEM((HIST_ROWS + TQ, D_POOL), F32)],
        compiler_params=pltpu.CompilerParams(dimension_semantics=("arbitrary", "arbitrary"),
                                             vmem_limit_bytes=VMEM_LIMIT_BYTES),
        name="prompt_layer",
    )(x, p, table, gains, pscale, *weights)


def _sample_kernel(n_seq, x_ref, p_ref, ck_ref, cv_ref, sp_ref, table_ref, gains_ref, pscale_ref, w_in_ref,
                   w_pool_ref, w_out_ref, w_ff1_ref, w_ff2_ref, w_ple_ref, w_gate_ref,
                   y_ref, knew_ref, vnew_ref, unew_ref, q_s, mix_s, kext, vt, uext):
    b = pl.program_id(0)
    row0 = pl.multiple_of(b * n_seq, n_seq)

    @pl.when(b == 0)
    def _():
        q, k, v, u = _project(x_ref[...], gains_ref, w_in_ref)
        q_s[...] = q
        knew_ref[...] = k
        vnew_ref[...] = v
        mix_s[:, D_ATTN:] = u
        kext[LEFT:SAMPLE_LK, :] = jnp.zeros((SAMPLE_TQ, D_ATTN), BF16)
        vt[:, LEFT:SAMPLE_LK] = jnp.zeros((D_ATTN, SAMPLE_TQ), BF16)

    pad = jnp.zeros((SAMPLE_TQ - n_seq, D_ATTN), F32)
    q = jnp.concatenate([q_s[pl.ds(row0, n_seq), :], pad], axis=0)
    v = jnp.concatenate([vnew_ref[pl.ds(row0, n_seq), :], pad], axis=0)
    u = mix_s[pl.ds(row0, n_seq), D_ATTN:]
    kext[0:LEFT, :] = ck_ref[0].astype(BF16)
    kext[LEFT:LEFT + n_seq, :] = knew_ref[pl.ds(row0, n_seq), :].astype(BF16)
    vt[:, 0:LEFT] = cv_ref[0].T.astype(BF16)
    vt[:, LEFT:SAMPLE_LK] = v.T.astype(BF16)
    uext[0:HIST_ROWS, :] = sp_ref[0]
    uext[HIST_ROWS:HIST_ROWS + n_seq, :] = u
    unew_ref[0] = u[n_seq - HIST_ROWS:, :]

    kk = lax.broadcasted_iota(jnp.int32, (SAMPLE_LK, SAMPLE_TQ), 0)
    attn = _attention_t(q.T, kext, vt, table_ref, kk < LEFT + n_seq).T
    pool = _pool(uext, n_seq, POOL_HIST, w_pool_ref, pscale_ref[...])
    mix_s[pl.ds(row0, n_seq), 0:D_ATTN] = attn[0:n_seq, :]
    mix_s[pl.ds(row0, n_seq), D_ATTN:] = pool

    @pl.when(b == pl.num_programs(0) - 1)
    def _():
        mix = mix_s[...]
        y_ref[...] = _finish(x_ref[...], mix[:, 0:D_ATTN], mix[:, D_ATTN:], p_ref[...], gains_ref, w_out_ref,
                             w_ff1_ref, w_ff2_ref, w_ple_ref, w_gate_ref)


def _sample_layer(x, p, cache_k, cache_v, state_pool, table, gains, pscale, weights):
    batch, n_seq, _ = x.shape
    rows = batch * n_seq
    assert cache_k.shape[1] == LEFT and HIST_ROWS <= n_seq <= SAMPLE_TQ and n_seq % 8 == 0
    full = lambda shape: pl.BlockSpec(shape, lambda b: (0,) * len(shape))
    per_b = lambda shape: pl.BlockSpec((1,) + shape, lambda b: (b, 0, 0))
    return pl.pallas_call(
        functools.partial(_sample_kernel, n_seq),
        grid=(batch,),
        in_specs=[full((rows, D_MODEL)), full((rows, D_PLE)), per_b((LEFT, D_ATTN)), per_b((LEFT, D_ATTN)),
                  per_b((HIST_ROWS, D_POOL)), full((N_HEADS, SAMPLE_LK, SAMPLE_TQ))]
        + [_whole()] * (2 + len(weights)),
        out_specs=[full((rows, D_MODEL)), full((rows, D_ATTN)), full((rows, D_ATTN)), per_b((HIST_ROWS, D_POOL))],
        out_shape=[jax.ShapeDtypeStruct((rows, D_MODEL), F32), jax.ShapeDtypeStruct((rows, D_ATTN), F32),
                   jax.ShapeDtypeStruct((rows, D_ATTN), F32),
                   jax.ShapeDtypeStruct((batch, HIST_ROWS, D_POOL), F32)],
        scratch_shapes=[pltpu.VMEM((rows, D_ATTN), F32), pltpu.VMEM((rows, D_ATTN + D_POOL), F32),
                        pltpu.VMEM((SAMPLE_LK, D_ATTN), BF16), pltpu.VMEM((D_ATTN, SAMPLE_LK), BF16),
                        pltpu.VMEM((HIST_ROWS + n_seq, D_POOL), F32)],
        compiler_params=pltpu.CompilerParams(dimension_semantics=("arbitrary",),
                                             vmem_limit_bytes=VMEM_LIMIT_BYTES),
        name="sample_layer",
    )(x.reshape(rows, D_MODEL), p.reshape(rows, D_PLE), cache_k, cache_v, state_pool, table, gains, pscale,
      *weights)


def kernel(x_prompt, x_sample, cache_k, cache_v, state_pool, p_prompt, p_sample, g_mix_pre, g_mix_post,
           g_ff_pre, g_ff_post, g_ple_post, w_in, rel_bias, w_pool, pool_scale, w_out, w_ff1, w_ff2, w_ple,
           w_ple_gate):
    depth = w_in.shape[0]
    batch, seq, _ = x_prompt.shape
    dec_batch, dec_seq, _ = x_sample.shape
    xp, xs = x_prompt, x_sample
    outs = [[] for _ in range(6)]
    for i in range(depth):
        table = _bias_table(rel_bias[i])
        gains = jnp.concatenate(
            [g_mix_pre[i:i + 1], g_mix_post[i:i + 1], g_ff_pre[i:i + 1], g_ff_post[i:i + 1], g_ple_post[i:i + 1],
             jnp.zeros((3, D_MODEL), F32)], axis=0)
        pscale = pool_scale[i:i + 1]
        weights = tuple(w[i].astype(BF16) for w in (w_in, w_pool, w_out, w_ff1, w_ff2, w_ple, w_ple_gate))
        xp, kp, vp, up = _prompt_layer(xp, p_prompt[i], table, gains, pscale, weights)
        sp = jnp.pad(state_pool[i], ((0, 0), (HIST_ROWS - POOL_HIST, 0), (0, 0)))
        xs2, kn, vn, un = _sample_layer(xs, p_sample[i], cache_k[i].reshape(dec_batch, LEFT, D_ATTN),
                                        cache_v[i].reshape(dec_batch, LEFT, D_ATTN), sp, table, gains, pscale,
                                        weights)
        xs = xs2.reshape(dec_batch, dec_seq, D_MODEL)
        outs[0].append(kp.reshape(batch, LEFT, N_HEADS, HEAD_DIM))
        outs[1].append(vp.reshape(batch, LEFT, N_HEADS, HEAD_DIM))
        outs[2].append(up[:, HIST_ROWS - POOL_HIST:, :])
        outs[3].append(kn.reshape(dec_batch, dec_seq, N_HEADS, HEAD_DIM))
        outs[4].append(vn.reshape(dec_batch, dec_seq, N_HEADS, HEAD_DIM))
        outs[5].append(un[:, HIST_ROWS - POOL_HIST:, :])
    return (xp, xs) + tuple(jnp.stack(o) for o in outs)
```

```python
import functools

import jax
import jax.numpy as jnp
from jax import lax
from jax.experimental import pallas as pl
from jax.experimental.pallas import tpu as pltpu

D_MODEL = 1024
D_ATTN = 512
D_POOL = 512
HEAD_DIM = 64
N_HEADS = 8
CHUNK = 64
LEFT_CHUNKS = 8
LEFT = LEFT_CHUNKS * CHUNK
REL_CLIP = 128
N_REL = 2 * REL_CLIP + 1
POOL_WINDOWS = (2, 4, 8, 16)
POOL_GROUP_DIM = 128
POOL_HIST = 15
HIST_ROWS = 16
D_FF = 4096
FF_CHUNK = 1024
D_PLE = 256
EPS = 1e-6
NEG_INF = -1e30
ATTN_SCALE = HEAD_DIM ** -0.5
LOG2E = 1.4426950408889634
V_ROWS = HEAD_DIM + 16

TQ = 256
LK = LEFT + TQ
SAMPLE_TQ = 128
SAMPLE_LK = LEFT + SAMPLE_TQ
TABLE_PAD = 256
VMEM_LIMIT_BYTES = 56 * 1024 * 1024

F32 = jnp.float32
BF16 = jnp.bfloat16


def _bias_table_kernel(rbt_ref, out_ref, buf_a, buf_b):
    far = jnp.broadcast_to(rbt_ref[0, 0:1, :], (TABLE_PAD + LEFT - REL_CLIP, TQ))
    buf_a[0:TABLE_PAD + LEFT - REL_CLIP, :] = far
    buf_b[0:TABLE_PAD, :] = far[0:TABLE_PAD]
    buf_a[TABLE_PAD + LEFT - REL_CLIP:TABLE_PAD + LEFT + REL_CLIP, :] = jnp.broadcast_to(
        rbt_ref[0, 0:2 * REL_CLIP, :], (2 * REL_CLIP, TQ))
    buf_a[TABLE_PAD + LEFT + REL_CLIP:TABLE_PAD + LK, :] = jnp.broadcast_to(
        rbt_ref[0, 2 * REL_CLIP:2 * REL_CLIP + 1, :], (LK - LEFT - REL_CLIP, TQ))

    rows = 128
    qidx = lax.broadcasted_iota(jnp.int32, (rows, TQ), 1)
    src, dst = buf_a, buf_b
    for b in range(TQ.bit_length() - 1):
        s = 1 << b
        bit = (qidx & s) != 0
        for r0 in range(TABLE_PAD, TABLE_PAD + LK, rows):
            dst[r0:r0 + rows, :] = jnp.where(bit, src[r0 - s:r0 - s + rows, :], src[r0:r0 + rows, :])
        src, dst = dst, src

    kk = lax.broadcasted_iota(jnp.int32, (LK, TQ), 0)
    qq = lax.broadcasted_iota(jnp.int32, (LK, TQ), 1)
    d = (kk >> 6) - (qq >> 6)
    out_ref[0] = jnp.where((d >= 0) & (d <= LEFT_CHUNKS), src[TABLE_PAD:TABLE_PAD + LK, :] * LOG2E, NEG_INF)


def _bias_table(rel_bias):
    rbt = jnp.pad(rel_bias[:, ::-1], ((0, 0), (0, 384 - N_REL)))[:, :, None]
    return pl.pallas_call(
        _bias_table_kernel,
        grid=(N_HEADS,),
        in_specs=[pl.BlockSpec((1, 384, 1), lambda h: (h, 0, 0))],
        out_specs=pl.BlockSpec((1, LK, TQ), lambda h: (h, 0, 0)),
        out_shape=jax.ShapeDtypeStruct((N_HEADS, LK, TQ), F32),
        scratch_shapes=[pltpu.VMEM((TABLE_PAD + LK, TQ), F32), pltpu.VMEM((TABLE_PAD + LK, TQ), F32)],
        name="bias_table",
    )(rbt)


def _rmsnorm(x, g):
    y = x * lax.rsqrt(jnp.mean(x * x, axis=-1, keepdims=True) + EPS)
    return y * g


def _dot(a, b):
    return jnp.dot(a, b, preferred_element_type=F32)


def _project(x, gains_ref, w_in_ref):
    h = _rmsnorm(x, gains_ref[0:1, :]).astype(BF16)
    qkvu = _dot(h, w_in_ref[...])
    q = qkvu[:, 0:D_ATTN] * (ATTN_SCALE * LOG2E)
    k = qkvu[:, D_ATTN:2 * D_ATTN]
    v = qkvu[:, 2 * D_ATTN:3 * D_ATTN]
    u = qkvu[:, 3 * D_ATTN:]
    return q, k, v, u


def _attention_pair(pair, q_t, kext_ref, vt_ref, table_ref, valid):
    nq = q_t.shape[1]
    row = lax.broadcasted_iota(jnp.int32, (2 * HEAD_DIM, nq), 0)
    kp = kext_ref[:, 2 * HEAD_DIM * pair:2 * HEAD_DIM * (pair + 1)]
    qp = q_t[2 * HEAD_DIM * pair:2 * HEAD_DIM * (pair + 1), :]
    outs = []
    for e in range(2):
        h = 2 * pair + e
        own = (row < HEAD_DIM) if e == 0 else (row >= HEAD_DIM)
        w = jnp.where(own, qp, 0.0).astype(BF16)
        s = _dot(kp, w)
        s = jnp.where(valid, s + table_ref[h], NEG_INF)
        m = jnp.max(s, axis=0, keepdims=True)
        p = jnp.exp2(s - m)
        l = jnp.sum(p, axis=0, keepdims=True)
        o = _dot(vt_ref[HEAD_DIM * h:HEAD_DIM * (h + 1), :], p.astype(BF16))
        outs.append(o / l)
    return outs


def _attention_t(q_t, kext_ref, vt_ref, table_ref, valid):
    outs = []
    for pair in range(N_HEADS // 2):
        outs += _attention_pair(pair, q_t, kext_ref, vt_ref, table_ref, valid)
    return jnp.concatenate(outs, axis=0)


def _pool(uext_ref, n_rows, frames_before, w_pool_ref, pool_scale):
    rows = lax.broadcasted_iota(jnp.int32, (n_rows, POOL_GROUP_DIM), 0)
    outs = []
    for g, w in enumerate(POOL_WINDOWS):
        lo = g * POOL_GROUP_DIM
        cur = uext_ref[HIST_ROWS:HIST_ROWS + n_rows, lo:lo + POOL_GROUP_DIM]
        acc = cur
        for j in range(1, w):
            acc = acc + uext_ref[HIST_ROWS - j:HIST_ROWS - j + n_rows, lo:lo + POOL_GROUP_DIM]
        cnt = jnp.minimum(w, frames_before + rows + 1).astype(F32)
        diff = acc / cnt - cur
        outs.append(_dot(diff.astype(BF16), w_pool_ref[g]))
    return jnp.concatenate(outs, axis=-1) * pool_scale


def _mix_out(x, mixed, gains_ref, w_out_ref):
    x = x + _rmsnorm(_dot(mixed, w_out_ref[...]), gains_ref[1:2, :])
    return x, _rmsnorm(x, gains_ref[2:3, :]).astype(BF16)


def _ffn_chunk(c, hn, w_ff1_ref, w_ff2_ref):
    act = jnp.square(jnp.maximum(_dot(hn, w_ff1_ref[c]), 0.0)).astype(BF16)
    return _dot(act, w_ff2_ref[c])


def _tail(x, ff, p, gains_ref, w_ple_ref, w_gate_ref):
    x = x + _rmsnorm(ff, gains_ref[3:4, :])
    gate = jax.nn.sigmoid(_dot(x.astype(BF16), w_gate_ref[...]))
    ple = gate * _dot(p.astype(BF16), w_ple_ref[...])
    return x + _rmsnorm(ple, gains_ref[4:5, :])


def _finish(x, mixed, p, gains_ref, w_out_ref, w_ff1_ref, w_ff2_ref, w_ple_ref, w_gate_ref):
    x, hn = _mix_out(x, mixed, gains_ref, w_out_ref)
    ff = None
    for c in range(D_FF // FF_CHUNK):
        part = _ffn_chunk(c, hn, w_ff1_ref, w_ff2_ref)
        ff = part if ff is None else ff + part
    return _tail(x, ff, p, gains_ref, w_ple_ref, w_gate_ref)


def _mask_lane(h):
    return HEAD_DIM if h % 2 == 0 else 0


def _prompt_kernel(n_tiles, n_total, x_ref, p_ref, table_ref, gains_ref, pscale_ref, w_in_ref, w_pool_ref,
                   w_out_ref, w_ff1_ref, w_ff2_ref, w_ple_ref, w_gate_ref,
                   y_ref, knew_ref, vnew_ref, unew_ref,
                   kext, vt, uext, xbuf, mixbuf, x1_s, ff_s, wq_s, s_scr, p_scr):
    s = pl.program_id(0)
    t = lax.rem(jnp.minimum(s, n_total - 1), n_tiles)

    @pl.when(s == 0)
    def _():
        xbuf[...] = jnp.zeros_like(xbuf)
        mixbuf[...] = jnp.zeros_like(mixbuf)
        x1_s[...] = jnp.zeros_like(x1_s)
        ff_s[...] = jnp.zeros_like(ff_s)

    @pl.when(t == 0)
    def _():
        lane = lax.broadcasted_iota(jnp.int32, (LK, 2 * HEAD_DIM), 1)
        for h in range(N_HEADS):
            kext[h] = jnp.where(lane == _mask_lane(h), NEG_INF, 0.0).astype(BF16)
        vt[...] = jnp.zeros_like(vt)
        uext[0:HIST_ROWS, :] = jnp.zeros((HIST_ROWS, D_POOL), F32)

    y_ref[0] = _tail(x1_s[...], ff_s[...], p_ref[0], gains_ref, w_ple_ref, w_gate_ref)

    q, k, v, u = _project(x_ref[0], gains_ref, w_in_ref)
    knew_ref[0] = k
    vnew_ref[0] = v
    unew_ref[0] = u[TQ - HIST_ROWS:, :]
    uext[HIST_ROWS:HIST_ROWS + TQ, :] = u
    q_t = q.T
    v_t = v.T.astype(BF16)
    row = lax.broadcasted_iota(jnp.int32, (2 * HEAD_DIM, TQ), 0)
    lane = lax.broadcasted_iota(jnp.int32, (TQ, 2 * HEAD_DIM), 1)
    ones_rows = jnp.where(lax.broadcasted_iota(jnp.int32, (V_ROWS - HEAD_DIM, TQ), 0) == 0, 1.0, 0.0).astype(BF16)
    for h in range(N_HEADS):
        lanes = slice(2 * HEAD_DIM * (h // 2), 2 * HEAD_DIM * (h // 2 + 1))
        own = (lane < HEAD_DIM) if h % 2 == 0 else (lane >= HEAD_DIM)
        kext[h, 0:LEFT, :] = kext[h, TQ:LK, :]
        kext[h, LEFT:LK, :] = jnp.where(own, k[:, lanes], 0.0).astype(BF16)
        wq_s[h] = jnp.where(row == _mask_lane(h), 1.0, q_t[lanes, :]).astype(BF16)
        vt[h, :, 0:LEFT] = vt[h, :, TQ:LK]
        vt[h, :, LEFT:LK] = jnp.concatenate([v_t[HEAD_DIM * h:HEAD_DIM * (h + 1), :], ones_rows], axis=0)

    x_prev, hn = _mix_out(xbuf[...], mixbuf[...], gains_ref, w_out_ref)
    x1_s[...] = x_prev

    pool = _pool(uext, TQ, t * TQ, w_pool_ref, pscale_ref[...])
    uext[0:HIST_ROWS, :] = uext[TQ:TQ + HIST_ROWS, :]
    mixbuf[:, D_ATTN:] = pool.astype(BF16)

    def scores(h):
        half = LK // 2
        s_scr[h % 2, 0:half, :] = _dot(kext[h, 0:half, :], wq_s[h])
        s_scr[h % 2, half:LK, :] = _dot(kext[h, half:LK, :], wq_s[h])

    def values(h):
        o = _dot(vt[h], p_scr[h % 2])
        return o[0:HEAD_DIM, :] / o[HEAD_DIM:HEAD_DIM + 1, :]

    scores(0)
    ff = None
    heads = []
    for h in range(N_HEADS):
        if h + 1 < N_HEADS:
            scores(h + 1)
        if h > 0:
            heads.append(values(h - 1))
        sc = s_scr[h % 2] + table_ref[h]
        p_scr[h % 2] = jnp.exp2(sc - jnp.max(sc, axis=0, keepdims=True)).astype(BF16)
        if h % 2 == 0:
            part = _ffn_chunk(h // 2, hn, w_ff1_ref, w_ff2_ref)
            ff = part if ff is None else ff + part
    heads.append(values(N_HEADS - 1))
    ff_s[...] = ff
    mixbuf[:, 0:D_ATTN] = jnp.concatenate(heads, axis=0).T.astype(BF16)
    xbuf[...] = x_ref[0]


def _resident(a):
    return pl.BlockSpec(a.shape, lambda *_: (0,) * a.ndim, pipeline_mode=pl.Buffered(1))


def _prompt_layer(x, p, table, gains, pscale, weights):
    batch, seq, _ = x.shape
    assert seq % TQ == 0 and LEFT % TQ == 0 and seq >= LEFT and D_FF // FF_CHUNK == N_HEADS // 2
    n_tiles = seq // TQ
    n_total = batch * n_tiles
    keep_tiles = LEFT // TQ
    mixing = lambda s: (jnp.minimum(s, n_total - 1), 0, 0)
    finishing = lambda s: (jnp.maximum(s - 2, 0), 0, 0)

    def last(s):
        sa = jnp.minimum(s, n_total - 1)
        return (sa // n_tiles, jnp.maximum(lax.rem(sa, n_tiles) - (n_tiles - keep_tiles), 0), 0)

    y, k_new, v_new, u_new = pl.pallas_call(
        functools.partial(_prompt_kernel, n_tiles, n_total),
        grid=(n_total + 2,),
        in_specs=[pl.BlockSpec((1, TQ, D_MODEL), mixing), pl.BlockSpec((1, TQ, D_PLE), finishing)]
        + [_resident(a) for a in (table, gains, pscale) + tuple(weights)],
        out_specs=[pl.BlockSpec((1, TQ, D_MODEL), finishing), pl.BlockSpec((1, TQ, D_ATTN), last),
                   pl.BlockSpec((1, TQ, D_ATTN), last),
                   pl.BlockSpec((1, HIST_ROWS, D_POOL), lambda s: (jnp.minimum(s, n_total - 1) // n_tiles, 0, 0))],
        out_shape=[jax.ShapeDtypeStruct((n_total, TQ, D_MODEL), F32),
                   jax.ShapeDtypeStruct((batch, LEFT, D_ATTN), F32),
                   jax.ShapeDtypeStruct((batch, LEFT, D_ATTN), F32),
                   jax.ShapeDtypeStruct((batch, HIST_ROWS, D_POOL), F32)],
        scratch_shapes=[pltpu.VMEM((N_HEADS, LK, 2 * HEAD_DIM), BF16), pltpu.VMEM((N_HEADS, V_ROWS, LK), BF16),
                        pltpu.VMEM((HIST_ROWS + TQ, D_POOL), F32), pltpu.VMEM((TQ, D_MODEL), F32),
                        pltpu.VMEM((TQ, D_ATTN + D_POOL), BF16), pltpu.VMEM((TQ, D_MODEL), F32),
                        pltpu.VMEM((TQ, D_MODEL), F32), pltpu.VMEM((N_HEADS, 2 * HEAD_DIM, TQ), BF16),
                        pltpu.VMEM((2, LK, TQ), F32), pltpu.VMEM((2, LK, TQ), BF16)],
        compiler_params=pltpu.CompilerParams(dimension_semantics=("arbitrary",),
                                             vmem_limit_bytes=VMEM_LIMIT_BYTES),
        name="prompt_layer",
    )(x.reshape(n_total, TQ, D_MODEL), p.reshape(n_total, TQ, D_PLE), table, gains, pscale, *weights)
    return y.reshape(batch, seq, D_MODEL), k_new, v_new, u_new


def _sample_kernel(n_seq, x_ref, p_ref, ck_ref, cv_ref, sp_ref, table_ref, gains_ref, pscale_ref, w_in_ref,
                   w_pool_ref, w_out_ref, w_ff1_ref, w_ff2_ref, w_ple_ref, w_gate_ref,
                   y_ref, knew_ref, vnew_ref, unew_ref, q_s, mix_s, kext, vt, uext):
    b = pl.program_id(0)
    row0 = pl.multiple_of(b * n_seq, n_seq)

    @pl.when(b == 0)
    def _():
        q, k, v, u = _project(x_ref[...], gains_ref, w_in_ref)
        q_s[...] = q
        knew_ref[...] = k
        vnew_ref[...] = v
        mix_s[:, D_ATTN:] = u
        kext[LEFT:SAMPLE_LK, :] = jnp.zeros((SAMPLE_TQ, D_ATTN), BF16)
        vt[:, LEFT:SAMPLE_LK] = jnp.zeros((D_ATTN, SAMPLE_TQ), BF16)

    pad = jnp.zeros((SAMPLE_TQ - n_seq, D_ATTN), F32)
    q = jnp.concatenate([q_s[pl.ds(row0, n_seq), :], pad], axis=0)
    v = jnp.concatenate([vnew_ref[pl.ds(row0, n_seq), :], pad], axis=0)
    u = mix_s[pl.ds(row0, n_seq), D_ATTN:]
    kext[0:LEFT, :] = ck_ref[0].astype(BF16)
    kext[LEFT:LEFT + n_seq, :] = knew_ref[pl.ds(row0, n_seq), :].astype(BF16)
    vt[:, 0:LEFT] = cv_ref[0].T.astype(BF16)
    vt[:, LEFT:SAMPLE_LK] = v.T.astype(BF16)
    uext[0:HIST_ROWS, :] = sp_ref[0]
    uext[HIST_ROWS:HIST_ROWS + n_seq, :] = u
    unew_ref[0] = u[n_seq - HIST_ROWS:, :]

    kk = lax.broadcasted_iota(jnp.int32, (SAMPLE_LK, SAMPLE_TQ), 0)
    attn = _attention_t(q.T, kext, vt, table_ref, kk < LEFT + n_seq).T
    pool = _pool(uext, n_seq, POOL_HIST, w_pool_ref, pscale_ref[...])
    mix_s[pl.ds(row0, n_seq), 0:D_ATTN] = attn[0:n_seq, :]
    mix_s[pl.ds(row0, n_seq), D_ATTN:] = pool

    @pl.when(b == pl.num_programs(0) - 1)
    def _():
        y_ref[...] = _finish(x_ref[...], mix_s[...].astype(BF16), p_ref[...], gains_ref, w_out_ref, w_ff1_ref,
                             w_ff2_ref, w_ple_ref, w_gate_ref)


def _sample_layer(x, p, cache_k, cache_v, state_pool, table, gains, pscale, weights):
    batch, n_seq, _ = x.shape
    rows = batch * n_seq
    assert cache_k.shape[1] == LEFT and HIST_ROWS <= n_seq <= SAMPLE_TQ and n_seq % 8 == 0
    full = lambda shape: pl.BlockSpec(shape, lambda b: (0,) * len(shape))
    per_b = lambda shape: pl.BlockSpec((1,) + shape, lambda b: (b, 0, 0))
    return pl.pallas_call(
        functools.partial(_sample_kernel, n_seq),
        grid=(batch,),
        in_specs=[full((rows, D_MODEL)), full((rows, D_PLE)), per_b((LEFT, D_ATTN)), per_b((LEFT, D_ATTN)),
                  per_b((HIST_ROWS, D_POOL)), full((N_HEADS, SAMPLE_LK, SAMPLE_TQ))]
        + [_resident(a) for a in (gains, pscale) + tuple(weights)],
        out_specs=[full((rows, D_MODEL)), full((rows, D_ATTN)), full((rows, D_ATTN)), per_b((HIST_ROWS, D_POOL))],
        out_shape=[jax.ShapeDtypeStruct((rows, D_MODEL), F32), jax.ShapeDtypeStruct((rows, D_ATTN), F32),
                   jax.ShapeDtypeStruct((rows, D_ATTN), F32),
                   jax.ShapeDtypeStruct((batch, HIST_ROWS, D_POOL), F32)],
        scratch_shapes=[pltpu.VMEM((rows, D_ATTN), F32), pltpu.VMEM((rows, D_ATTN + D_POOL), F32),
                        pltpu.VMEM((SAMPLE_LK, D_ATTN), BF16), pltpu.VMEM((D_ATTN, SAMPLE_LK), BF16),
                        pltpu.VMEM((HIST_ROWS + n_seq, D_POOL), F32)],
        compiler_params=pltpu.CompilerParams(dimension_semantics=("arbitrary",),
                                             vmem_limit_bytes=VMEM_LIMIT_BYTES),
        name="sample_layer",
    )(x.reshape(rows, D_MODEL), p.reshape(rows, D_PLE), cache_k, cache_v, state_pool, table, gains, pscale,
      *weights)


def kernel(x_prompt, x_sample, cache_k, cache_v, state_pool, p_prompt, p_sample, g_mix_pre, g_mix_post,
           g_ff_pre, g_ff_post, g_ple_post, w_in, rel_bias, w_pool, pool_scale, w_out, w_ff1, w_ff2, w_ple,
           w_ple_gate):
    depth = w_in.shape[0]
    batch, seq, _ = x_prompt.shape
    dec_batch, dec_seq, _ = x_sample.shape
    xp, xs = x_prompt, x_sample
    outs = [[] for _ in range(6)]
    for i in range(depth):
        table = _bias_table(rel_bias[i])
        gains = jnp.concatenate(
            [g_mix_pre[i:i + 1], g_mix_post[i:i + 1], g_ff_pre[i:i + 1], g_ff_post[i:i + 1], g_ple_post[i:i + 1],
             jnp.zeros((3, D_MODEL), F32)], axis=0)
        pscale = pool_scale[i:i + 1]
        n_chunks = D_FF // FF_CHUNK
        ff1 = w_ff1[i].astype(BF16).reshape(D_MODEL, n_chunks, FF_CHUNK).transpose(1, 0, 2)
        ff2 = w_ff2[i].astype(BF16).reshape(n_chunks, FF_CHUNK, D_MODEL)
        weights = (w_in[i].astype(BF16), w_pool[i].astype(BF16), w_out[i].astype(BF16), ff1, ff2,
                   w_ple[i].astype(BF16), w_ple_gate[i].astype(BF16))
        xp, kp, vp, up = _prompt_layer(xp, p_prompt[i], table, gains, pscale, weights)
        sp = jnp.pad(state_pool[i], ((0, 0), (HIST_ROWS - POOL_HIST, 0), (0, 0)))
        xs2, kn, vn, un = _sample_layer(xs, p_sample[i], cache_k[i].reshape(dec_batch, LEFT, D_ATTN),
                                        cache_v[i].reshape(dec_batch, LEFT, D_ATTN), sp, table, gains, pscale,
                                        weights)
        xs = xs2.reshape(dec_batch, dec_seq, D_MODEL)
        outs[0].append(kp.reshape(batch, LEFT, N_HEADS, HEAD_DIM))
        outs[1].append(vp.reshape(batch, LEFT, N_HEADS, HEAD_DIM))
        outs[2].append(up[:, HIST_ROWS - POOL_HIST:, :])
        outs[3].append(kn.reshape(dec_batch, dec_seq, N_HEADS, HEAD_DIM))
        outs[4].append(vn.reshape(dec_batch, dec_seq, N_HEADS, HEAD_DIM))
        outs[5].append(un[:, HIST_ROWS - POOL_HIST:, :])
    return (xp, xs) + tuple(jnp.stack(o) for o in outs)
```

```python
import functools

import jax
import jax.numpy as jnp
from jax import lax
from jax.experimental import pallas as pl
from jax.experimental.pallas import tpu as pltpu

D_MODEL = 1024
D_ATTN = 512
D_POOL = 512
HEAD_DIM = 64
N_HEADS = 8
CHUNK = 64
LEFT_CHUNKS = 8
LEFT = LEFT_CHUNKS * CHUNK
REL_CLIP = 128
N_REL = 2 * REL_CLIP + 1
POOL_WINDOWS = (2, 4, 8, 16)
POOL_GROUP_DIM = 128
POOL_HIST = 15
HIST_ROWS = 16
D_FF = 4096
FF_CHUNK = 1024
D_PLE = 256
EPS = 1e-6
NEG_INF = -1e30
ATTN_SCALE = HEAD_DIM ** -0.5
LOG2E = 1.4426950408889634
V_ROWS = HEAD_DIM + 16

TQ = 256
LK = LEFT + TQ
SAMPLE_TQ = 128
SAMPLE_LK = LEFT + SAMPLE_TQ
TABLE_PAD = 256
VMEM_LIMIT_BYTES = 56 * 1024 * 1024
POOL_ROUND = 3
TILES_PER_STEP = 2

F32 = jnp.float32
BF16 = jnp.bfloat16


def _bias_table_kernel(rbt_ref, out_ref, buf_a, buf_b):
    far = jnp.broadcast_to(rbt_ref[0, 0:1, :], (TABLE_PAD + LEFT - REL_CLIP, TQ))
    buf_a[0:TABLE_PAD + LEFT - REL_CLIP, :] = far
    buf_b[0:TABLE_PAD, :] = far[0:TABLE_PAD]
    buf_a[TABLE_PAD + LEFT - REL_CLIP:TABLE_PAD + LEFT + REL_CLIP, :] = jnp.broadcast_to(
        rbt_ref[0, 0:2 * REL_CLIP, :], (2 * REL_CLIP, TQ))
    buf_a[TABLE_PAD + LEFT + REL_CLIP:TABLE_PAD + LK, :] = jnp.broadcast_to(
        rbt_ref[0, 2 * REL_CLIP:2 * REL_CLIP + 1, :], (LK - LEFT - REL_CLIP, TQ))

    rows = 128
    qidx = lax.broadcasted_iota(jnp.int32, (rows, TQ), 1)
    src, dst = buf_a, buf_b
    for b in range(TQ.bit_length() - 1):
        s = 1 << b
        bit = (qidx & s) != 0
        for r0 in range(TABLE_PAD, TABLE_PAD + LK, rows):
            dst[r0:r0 + rows, :] = jnp.where(bit, src[r0 - s:r0 - s + rows, :], src[r0:r0 + rows, :])
        src, dst = dst, src

    kk = lax.broadcasted_iota(jnp.int32, (LK, TQ), 0)
    qq = lax.broadcasted_iota(jnp.int32, (LK, TQ), 1)
    d = (kk >> 6) - (qq >> 6)
    out_ref[0] = jnp.where((d >= 0) & (d <= LEFT_CHUNKS), src[TABLE_PAD:TABLE_PAD + LK, :] * LOG2E, NEG_INF)


def _bias_table(rel_bias):
    rbt = jnp.pad(rel_bias[:, ::-1], ((0, 0), (0, 384 - N_REL)))[:, :, None]
    return pl.pallas_call(
        _bias_table_kernel,
        grid=(N_HEADS,),
        in_specs=[pl.BlockSpec((1, 384, 1), lambda h: (h, 0, 0))],
        out_specs=pl.BlockSpec((1, LK, TQ), lambda h: (h, 0, 0)),
        out_shape=jax.ShapeDtypeStruct((N_HEADS, LK, TQ), F32),
        scratch_shapes=[pltpu.VMEM((TABLE_PAD + LK, TQ), F32), pltpu.VMEM((TABLE_PAD + LK, TQ), F32)],
        name="bias_table",
    )(rbt)


def _rmsnorm(x, g):
    y = x * lax.rsqrt(jnp.mean(x * x, axis=-1, keepdims=True) + EPS)
    return y * g


def _dot(a, b):
    return jnp.dot(a, b, preferred_element_type=F32)


def _project(x, gains_ref, w_in_ref):
    h = _rmsnorm(x, gains_ref[0:1, :]).astype(BF16)
    qkvu = _dot(h, w_in_ref[...])
    q = qkvu[:, 0:D_ATTN] * (ATTN_SCALE * LOG2E)
    k = qkvu[:, D_ATTN:2 * D_ATTN]
    v = qkvu[:, 2 * D_ATTN:3 * D_ATTN]
    u = qkvu[:, 3 * D_ATTN:]
    return q, k, v, u


def _attention_pair(pair, q_t, kext_ref, vt_ref, table_ref, valid):
    nq = q_t.shape[1]
    row = lax.broadcasted_iota(jnp.int32, (2 * HEAD_DIM, nq), 0)
    kp = kext_ref[:, 2 * HEAD_DIM * pair:2 * HEAD_DIM * (pair + 1)]
    qp = q_t[2 * HEAD_DIM * pair:2 * HEAD_DIM * (pair + 1), :]
    outs = []
    for e in range(2):
        h = 2 * pair + e
        own = (row < HEAD_DIM) if e == 0 else (row >= HEAD_DIM)
        w = jnp.where(own, qp, 0.0).astype(BF16)
        s = _dot(kp, w)
        s = jnp.where(valid, s + table_ref[h], NEG_INF)
        m = jnp.max(s, axis=0, keepdims=True)
        p = jnp.exp2(s - m)
        l = jnp.sum(p, axis=0, keepdims=True)
        o = _dot(vt_ref[HEAD_DIM * h:HEAD_DIM * (h + 1), :], p.astype(BF16))
        outs.append(o / l)
    return outs


def _attention_t(q_t, kext_ref, vt_ref, table_ref, valid):
    outs = []
    for pair in range(N_HEADS // 2):
        outs += _attention_pair(pair, q_t, kext_ref, vt_ref, table_ref, valid)
    return jnp.concatenate(outs, axis=0)


def _pool(uext_ref, n_rows, frames_before, w_pool_ref, pool_scale):
    rows = lax.broadcasted_iota(jnp.int32, (n_rows, POOL_GROUP_DIM), 0)
    outs = []
    for g, w in enumerate(POOL_WINDOWS):
        lo = g * POOL_GROUP_DIM
        cur = uext_ref[HIST_ROWS:HIST_ROWS + n_rows, lo:lo + POOL_GROUP_DIM]
        acc = cur
        for j in range(1, w):
            acc = acc + uext_ref[HIST_ROWS - j:HIST_ROWS - j + n_rows, lo:lo + POOL_GROUP_DIM]
        cnt = jnp.minimum(w, frames_before + rows + 1).astype(F32)
        diff = acc / cnt - cur
        outs.append(_dot(diff.astype(BF16), w_pool_ref[g]))
    return jnp.concatenate(outs, axis=-1) * pool_scale


def _mix_out(x, mix, gains_ref):
    x = x + _rmsnorm(mix, gains_ref[1:2, :])
    return x, _rmsnorm(x, gains_ref[2:3, :]).astype(BF16)


def _ffn_chunk(c, hn, w_ff1_ref, w_ff2_ref):
    act = jnp.square(jnp.maximum(_dot(hn, w_ff1_ref[c]), 0.0)).astype(BF16)
    return _dot(act, w_ff2_ref[c])


def _tail(x, ff, p, gains_ref, w_ple_ref, w_gate_ref):
    x = x + _rmsnorm(ff, gains_ref[3:4, :])
    gate = jax.nn.sigmoid(_dot(x.astype(BF16), w_gate_ref[...]))
    ple = gate * _dot(p.astype(BF16), w_ple_ref[...])
    return x + _rmsnorm(ple, gains_ref[4:5, :])


def _finish(x, mixed, p, gains_ref, w_out_ref, w_ff1_ref, w_ff2_ref, w_ple_ref, w_gate_ref):
    x, hn = _mix_out(x, _dot(mixed, w_out_ref[...]), gains_ref)
    ff = None
    for c in range(D_FF // FF_CHUNK):
        part = _ffn_chunk(c, hn, w_ff1_ref, w_ff2_ref)
        ff = part if ff is None else ff + part
    return _tail(x, ff, p, gains_ref, w_ple_ref, w_gate_ref)


def _mask_lane(h):
    return HEAD_DIM if h % 2 == 0 else 0


def _prompt_tile(n_tiles, n_total, s, x_ref, p_ref, table_ref, gains_ref, pscale_ref, w_in_ref, w_pool_ref,
                 w_out_ref, w_ff1_ref, w_ff2_ref, w_ple_ref, w_gate_ref,
                 y_ref, knew_ref, vnew_ref, unew_ref,
                 kext, vt, uext, xbuf, mixbuf, x1_s, ff_s, wq_s, s_scr, p_scr):
    t = lax.rem(jnp.minimum(s, n_total - 1), n_tiles)

    @pl.when(s == 0)
    def _():
        xbuf[...] = jnp.zeros_like(xbuf)
        mixbuf[...] = jnp.zeros_like(mixbuf)
        x1_s[...] = jnp.zeros_like(x1_s)
        ff_s[...] = jnp.zeros_like(ff_s)

    @pl.when(t == 0)
    def _():
        lane = lax.broadcasted_iota(jnp.int32, (LK, 2 * HEAD_DIM), 1)
        for h in range(N_HEADS):
            kext[h] = jnp.where(lane == _mask_lane(h), NEG_INF, 0.0).astype(BF16)
        vt[...] = jnp.zeros_like(vt)
        uext[0:HIST_ROWS, :] = jnp.zeros((HIST_ROWS, D_POOL), F32)

    mix = _dot(mixbuf[...], w_out_ref[...])

    y_ref[0] = _tail(x1_s[...], ff_s[...], p_ref[0], gains_ref, w_ple_ref, w_gate_ref)

    x_prev, hn = _mix_out(xbuf[...], mix, gains_ref)
    x1_s[...] = x_prev

    def ffn_up(c):
        return jnp.square(jnp.maximum(_dot(hn, w_ff1_ref[c]), 0.0)).astype(BF16)

    def ffn_down(c, act):
        return _dot(act, w_ff2_ref[c])

    q, k, v, u = _project(x_ref[0], gains_ref, w_in_ref)
    act = ffn_up(0)
    knew_ref[0] = k
    vnew_ref[0] = v
    unew_ref[0] = u[TQ - HIST_ROWS:, :]
    uext[HIST_ROWS:HIST_ROWS + TQ, :] = u
    q_t = q.T
    v_t = v.T.astype(BF16)
    row = lax.broadcasted_iota(jnp.int32, (2 * HEAD_DIM, TQ), 0)
    lane = lax.broadcasted_iota(jnp.int32, (TQ, 2 * HEAD_DIM), 1)
    ones_rows = jnp.where(lax.broadcasted_iota(jnp.int32, (V_ROWS - HEAD_DIM, TQ), 0) == 0, 1.0, 0.0).astype(BF16)
    for h in range(N_HEADS):
        lanes = slice(2 * HEAD_DIM * (h // 2), 2 * HEAD_DIM * (h // 2 + 1))
        own = (lane < HEAD_DIM) if h % 2 == 0 else (lane >= HEAD_DIM)
        kext[h, 0:LEFT, :] = kext[h, TQ:LK, :]
        kext[h, LEFT:LK, :] = jnp.where(own, k[:, lanes], 0.0).astype(BF16)
        wq_s[h] = jnp.where(row == _mask_lane(h), 1.0, q_t[lanes, :]).astype(BF16)
        vt[h, :, 0:LEFT] = vt[h, :, TQ:LK]
        vt[h, :, LEFT:LK] = jnp.concatenate([v_t[HEAD_DIM * h:HEAD_DIM * (h + 1), :], ones_rows], axis=0)

    def scores(h):
        half = LK // 2
        s_scr[h % 2, 0:half, :] = _dot(kext[h, 0:half, :], wq_s[h])
        s_scr[h % 2, half:LK, :] = _dot(kext[h, half:LK, :], wq_s[h])

    def values(h):
        o = _dot(vt[h], p_scr[h % 2])
        return o[0:HEAD_DIM, :] / o[HEAD_DIM:HEAD_DIM + 1, :]

    scores(0)
    scores(1)
    ff = None
    heads = []
    for h in range(N_HEADS):
        sc = s_scr[h % 2] + table_ref[h]
        p_scr[h % 2] = jnp.exp2(sc - jnp.max(sc, axis=0, keepdims=True)).astype(BF16)
        if h % 2 == 0:
            part = ffn_down(h // 2, act)
            ff = part if ff is None else ff + part
        elif h // 2 + 1 < D_FF // FF_CHUNK:
            act = ffn_up(h // 2 + 1)
        heads.append(values(h))
        if h + 2 < N_HEADS:
            scores(h + 2)
        if h == POOL_ROUND:
            pool = _pool(uext, TQ, t * TQ, w_pool_ref, pscale_ref[...])
            uext[0:HIST_ROWS, :] = uext[TQ:TQ + HIST_ROWS, :]
            mixbuf[:, D_ATTN:] = pool.astype(BF16)
    ff_s[...] = ff
    mixbuf[:, 0:D_ATTN] = jnp.concatenate(heads, axis=0).T.astype(BF16)
    xbuf[...] = x_ref[0]


def _prompt_kernel(n_tiles, n_total, x_ref, p_ref, *refs):
    n_const = 10
    consts, (y_ref, knew_ref, vnew_ref, unew_ref), scratch = refs[:n_const], refs[n_const:n_const + 4], refs[n_const + 4:]

    def tile_step(j, carry):
        rows = pl.ds(pl.multiple_of(j * TQ, TQ), TQ)
        _prompt_tile(n_tiles, n_total, TILES_PER_STEP * pl.program_id(0) + j, x_ref.at[pl.ds(j, 1)],
                     p_ref.at[pl.ds(j, 1)], *consts, y_ref.at[pl.ds(j, 1)], knew_ref.at[:, rows],
                     vnew_ref.at[:, rows], unew_ref, *scratch)
        return carry

    lax.fori_loop(0, TILES_PER_STEP, tile_step, 0)


def _resident(a):
    return pl.BlockSpec(a.shape, lambda *_: (0,) * a.ndim, pipeline_mode=pl.Buffered(1))


def _prompt_layer(x, p, table, gains, pscale, weights):
    batch, seq, _ = x.shape
    assert seq % TQ == 0 and seq >= LEFT and D_FF // FF_CHUNK == N_HEADS // 2
    assert LEFT == TILES_PER_STEP * TQ and (seq // TQ) % TILES_PER_STEP == 0
    n_tiles = seq // TQ
    n_total = batch * n_tiles
    n_blocks = n_total // TILES_PER_STEP
    mixing = lambda g: (jnp.minimum(g, n_blocks - 1), 0, 0)
    finishing = lambda g: (jnp.maximum(g - 1, 0), 0, 0)
    per_batch = lambda g: (jnp.minimum(g, n_blocks - 1) * TILES_PER_STEP // n_tiles, 0, 0)
    consts = (table, gains, pscale) + tuple(weights)
    y, k_new, v_new, u_new = pl.pallas_call(
        functools.partial(_prompt_kernel, n_tiles, n_total),
        grid=(n_blocks + 1,),
        in_specs=[pl.BlockSpec((TILES_PER_STEP, TQ, D_MODEL), mixing),
                  pl.BlockSpec((TILES_PER_STEP, TQ, D_PLE), finishing)] + [_resident(a) for a in consts],
        out_specs=[pl.BlockSpec((TILES_PER_STEP, TQ, D_MODEL), finishing), pl.BlockSpec((1, LEFT, D_ATTN), per_batch),
                   pl.BlockSpec((1, LEFT, D_ATTN), per_batch), pl.BlockSpec((1, HIST_ROWS, D_POOL), per_batch)],
        out_shape=[jax.ShapeDtypeStruct((n_total, TQ, D_MODEL), F32),
                   jax.ShapeDtypeStruct((batch, LEFT, D_ATTN), F32),
                   jax.ShapeDtypeStruct((batch, LEFT, D_ATTN), F32),
                   jax.ShapeDtypeStruct((batch, HIST_ROWS, D_POOL), F32)],
        scratch_shapes=[pltpu.VMEM((N_HEADS, LK, 2 * HEAD_DIM), BF16), pltpu.VMEM((N_HEADS, V_ROWS, LK), BF16),
                        pltpu.VMEM((HIST_ROWS + TQ, D_POOL), F32), pltpu.VMEM((TQ, D_MODEL), F32),
                        pltpu.VMEM((TQ, D_ATTN + D_POOL), BF16), pltpu.VMEM((TQ, D_MODEL), F32),
                        pltpu.VMEM((TQ, D_MODEL), F32), pltpu.VMEM((N_HEADS, 2 * HEAD_DIM, TQ), BF16),
                        pltpu.VMEM((2, LK, TQ), F32), pltpu.VMEM((2, LK, TQ), BF16)],
        compiler_params=pltpu.CompilerParams(dimension_semantics=("arbitrary",),
                                             vmem_limit_bytes=VMEM_LIMIT_BYTES),
        name="prompt_layer",
    )(x.reshape(n_total, TQ, D_MODEL), p.reshape(n_total, TQ, D_PLE), *consts)
    return y.reshape(batch, seq, D_MODEL), k_new, v_new, u_new


def _sample_kernel(n_seq, x_ref, p_ref, ck_ref, cv_ref, sp_ref, table_ref, gains_ref, pscale_ref, w_in_ref,
                   w_pool_ref, w_out_ref, w_ff1_ref, w_ff2_ref, w_ple_ref, w_gate_ref,
                   y_ref, knew_ref, vnew_ref, unew_ref, q_s, mix_s, kext, vt, uext):
    b = pl.program_id(0)
    row0 = pl.multiple_of(b * n_seq, n_seq)

    @pl.when(b == 0)
    def _():
        q, k, v, u = _project(x_ref[...], gains_ref, w_in_ref)
        q_s[...] = q
        knew_ref[...] = k
        vnew_ref[...] = v
        mix_s[:, D_ATTN:] = u
        kext[LEFT:SAMPLE_LK, :] = jnp.zeros((SAMPLE_TQ, D_ATTN), BF16)
        vt[:, LEFT:SAMPLE_LK] = jnp.zeros((D_ATTN, SAMPLE_TQ), BF16)

    pad = jnp.zeros((SAMPLE_TQ - n_seq, D_ATTN), F32)
    q = jnp.concatenate([q_s[pl.ds(row0, n_seq), :], pad], axis=0)
    v = jnp.concatenate([vnew_ref[pl.ds(row0, n_seq), :], pad], axis=0)
    u = mix_s[pl.ds(row0, n_seq), D_ATTN:]
    kext[0:LEFT, :] = ck_ref[0].astype(BF16)
    kext[LEFT:LEFT + n_seq, :] = knew_ref[pl.ds(row0, n_seq), :].astype(BF16)
    vt[:, 0:LEFT] = cv_ref[0].T.astype(BF16)
    vt[:, LEFT:SAMPLE_LK] = v.T.astype(BF16)
    uext[0:HIST_ROWS, :] = sp_ref[0]
    uext[HIST_ROWS:HIST_ROWS + n_seq, :] = u
    unew_ref[0] = u[n_seq - HIST_ROWS:, :]

    kk = lax.broadcasted_iota(jnp.int32, (SAMPLE_LK, SAMPLE_TQ), 0)
    attn = _attention_t(q.T, kext, vt, table_ref, kk < LEFT + n_seq).T
    pool = _pool(uext, n_seq, POOL_HIST, w_pool_ref, pscale_ref[...])
    mix_s[pl.ds(row0, n_seq), 0:D_ATTN] = attn[0:n_seq, :]
    mix_s[pl.ds(row0, n_seq), D_ATTN:] = pool

    @pl.when(b == pl.num_programs(0) - 1)
    def _():
        y_ref[...] = _finish(x_ref[...], mix_s[...].astype(BF16), p_ref[...], gains_ref, w_out_ref, w_ff1_ref,
                             w_ff2_ref, w_ple_ref, w_gate_ref)


def _sample_layer(x, p, cache_k, cache_v, state_pool, table, gains, pscale, weights):
    batch, n_seq, _ = x.shape
    rows = batch * n_seq
    assert cache_k.shape[1] == LEFT and HIST_ROWS <= n_seq <= SAMPLE_TQ and n_seq % 8 == 0
    full = lambda shape: pl.BlockSpec(shape, lambda b: (0,) * len(shape))
    per_b = lambda shape: pl.BlockSpec((1,) + shape, lambda b: (b, 0, 0))
    return pl.pallas_call(
        functools.partial(_sample_kernel, n_seq),
        grid=(batch,),
        in_specs=[full((rows, D_MODEL)), full((rows, D_PLE)), per_b((LEFT, D_ATTN)), per_b((LEFT, D_ATTN)),
                  per_b((HIST_ROWS, D_POOL)), full((N_HEADS, SAMPLE_LK, SAMPLE_TQ))]
        + [_resident(a) for a in (gains, pscale) + tuple(weights)],
        out_specs=[full((rows, D_MODEL)), full((rows, D_ATTN)), full((rows, D_ATTN)), per_b((HIST_ROWS, D_POOL))],
        out_shape=[jax.ShapeDtypeStruct((rows, D_MODEL), F32), jax.ShapeDtypeStruct((rows, D_ATTN), F32),
                   jax.ShapeDtypeStruct((rows, D_ATTN), F32),
                   jax.ShapeDtypeStruct((batch, HIST_ROWS, D_POOL), F32)],
        scratch_shapes=[pltpu.VMEM((rows, D_ATTN), F32), pltpu.VMEM((rows, D_ATTN + D_POOL), F32),
                        pltpu.VMEM((SAMPLE_LK, D_ATTN), BF16), pltpu.VMEM((D_ATTN, SAMPLE_LK), BF16),
                        pltpu.VMEM((HIST_ROWS + n_seq, D_POOL), F32)],
        compiler_params=pltpu.CompilerParams(dimension_semantics=("arbitrary",),
                                             vmem_limit_bytes=VMEM_LIMIT_BYTES),
        name="sample_layer",
    )(x.reshape(rows, D_MODEL), p.reshape(rows, D_PLE), cache_k, cache_v, state_pool, table, gains, pscale,
      *weights)


def kernel(x_prompt, x_sample, cache_k, cache_v, state_pool, p_prompt, p_sample, g_mix_pre, g_mix_post,
           g_ff_pre, g_ff_post, g_ple_post, w_in, rel_bias, w_pool, pool_scale, w_out, w_ff1, w_ff2, w_ple,
           w_ple_gate):
    depth = w_in.shape[0]
    batch, seq, _ = x_prompt.shape
    dec_batch, dec_seq, _ = x_sample.shape
    xp, xs = x_prompt, x_sample
    outs = [[] for _ in range(6)]
    for i in range(depth):
        table = _bias_table(rel_bias[i])
        gains = jnp.concatenate(
            [g_mix_pre[i:i + 1], g_mix_post[i:i + 1], g_ff_pre[i:i + 1], g_ff_post[i:i + 1], g_ple_post[i:i + 1],
             jnp.zeros((3, D_MODEL), F32)], axis=0)
        pscale = pool_scale[i:i + 1]
        n_chunks = D_FF // FF_CHUNK
        ff1 = w_ff1[i].astype(BF16).reshape(D_MODEL, n_chunks, FF_CHUNK).transpose(1, 0, 2)
        ff2 = w_ff2[i].astype(BF16).reshape(n_chunks, FF_CHUNK, D_MODEL)
        weights = (w_in[i].astype(BF16), w_pool[i].astype(BF16), w_out[i].astype(BF16), ff1, ff2,
                   w_ple[i].astype(BF16), w_ple_gate[i].astype(BF16))
        xp, kp, vp, up = _prompt_layer(xp, p_prompt[i], table, gains, pscale, weights)
        sp = jnp.pad(state_pool[i], ((0, 0), (HIST_ROWS - POOL_HIST, 0), (0, 0)))
        xs2, kn, vn, un = _sample_layer(xs, p_sample[i], cache_k[i].reshape(dec_batch, LEFT, D_ATTN),
                                        cache_v[i].reshape(dec_batch, LEFT, D_ATTN), sp, table, gains, pscale,
                                        weights)
        xs = xs2.reshape(dec_batch, dec_seq, D_MODEL)
        outs[0].append(kp.reshape(batch, LEFT, N_HEADS, HEAD_DIM))
        outs[1].append(vp.reshape(batch, LEFT, N_HEADS, HEAD_DIM))
        outs[2].append(up[:, HIST_ROWS - POOL_HIST:, :])
        outs[3].append(kn.reshape(dec_batch, dec_seq, N_HEADS, HEAD_DIM))
        outs[4].append(vn.reshape(dec_batch, dec_seq, N_HEADS, HEAD_DIM))
        outs[5].append(un[:, HIST_ROWS - POOL_HIST:, :])
    return (xp, xs) + tuple(jnp.stack(o) for o in outs)
```

```python
import functools

import jax
import jax.numpy as jnp
from jax import lax
from jax.experimental import pallas as pl
from jax.experimental.pallas import tpu as pltpu

D_MODEL = 1024
D_ATTN = 512
D_POOL = 512
HEAD_DIM = 64
N_HEADS = 8
CHUNK = 64
LEFT_CHUNKS = 8
LEFT = LEFT_CHUNKS * CHUNK
REL_CLIP = 128
N_REL = 2 * REL_CLIP + 1
POOL_WINDOWS = (2, 4, 8, 16)
POOL_GROUP_DIM = 128
POOL_HIST = 15
HIST_ROWS = 16
D_FF = 4096
FF_CHUNK = 1024
D_PLE = 256
EPS = 1e-6
NEG_INF = -1e30
ATTN_SCALE = HEAD_DIM ** -0.5
LOG2E = 1.4426950408889634
V_ROWS = HEAD_DIM + 16

TQ = 256
LK = LEFT + TQ
SAMPLE_TQ = 128
SAMPLE_LK = LEFT + SAMPLE_TQ
TABLE_PAD = 256
VMEM_LIMIT_BYTES = 56 * 1024 * 1024
POOL_ROUND = 3
TILES_PER_STEP = 2

F32 = jnp.float32
BF16 = jnp.bfloat16


def _bias_table_kernel(rbt_ref, out_ref, buf_a, buf_b):
    far = jnp.broadcast_to(rbt_ref[0, 0:1, :], (TABLE_PAD + LEFT - REL_CLIP, TQ))
    buf_a[0:TABLE_PAD + LEFT - REL_CLIP, :] = far
    buf_b[0:TABLE_PAD, :] = far[0:TABLE_PAD]
    buf_a[TABLE_PAD + LEFT - REL_CLIP:TABLE_PAD + LEFT + REL_CLIP, :] = jnp.broadcast_to(
        rbt_ref[0, 0:2 * REL_CLIP, :], (2 * REL_CLIP, TQ))
    buf_a[TABLE_PAD + LEFT + REL_CLIP:TABLE_PAD + LK, :] = jnp.broadcast_to(
        rbt_ref[0, 2 * REL_CLIP:2 * REL_CLIP + 1, :], (LK - LEFT - REL_CLIP, TQ))

    rows = 128
    qidx = lax.broadcasted_iota(jnp.int32, (rows, TQ), 1)
    src, dst = buf_a, buf_b
    for b in range(TQ.bit_length() - 1):
        s = 1 << b
        bit = (qidx & s) != 0
        for r0 in range(TABLE_PAD, TABLE_PAD + LK, rows):
            dst[r0:r0 + rows, :] = jnp.where(bit, src[r0 - s:r0 - s + rows, :], src[r0:r0 + rows, :])
        src, dst = dst, src

    kk = lax.broadcasted_iota(jnp.int32, (LK, TQ), 0)
    qq = lax.broadcasted_iota(jnp.int32, (LK, TQ), 1)
    d = (kk >> 6) - (qq >> 6)
    out_ref[0] = jnp.where((d >= 0) & (d <= LEFT_CHUNKS), src[TABLE_PAD:TABLE_PAD + LK, :] * LOG2E, NEG_INF)


def _bias_table(rel_bias):
    rbt = jnp.pad(rel_bias[:, ::-1], ((0, 0), (0, 384 - N_REL)))[:, :, None]
    return pl.pallas_call(
        _bias_table_kernel,
        grid=(N_HEADS,),
        in_specs=[pl.BlockSpec((1, 384, 1), lambda h: (h, 0, 0))],
        out_specs=pl.BlockSpec((1, LK, TQ), lambda h: (h, 0, 0)),
        out_shape=jax.ShapeDtypeStruct((N_HEADS, LK, TQ), F32),
        scratch_shapes=[pltpu.VMEM((TABLE_PAD + LK, TQ), F32), pltpu.VMEM((TABLE_PAD + LK, TQ), F32)],
        name="bias_table",
    )(rbt)


def _rmsnorm(x, g):
    y = x * lax.rsqrt(jnp.mean(x * x, axis=-1, keepdims=True) + EPS)
    return y * g


def _dot(a, b):
    return jnp.dot(a, b, preferred_element_type=F32)


def _project(x, gains_ref, w_in_ref):
    h = _rmsnorm(x, gains_ref[0:1, :]).astype(BF16)
    qkvu = _dot(h, w_in_ref[...])
    q = qkvu[:, 0:D_ATTN] * (ATTN_SCALE * LOG2E)
    k = qkvu[:, D_ATTN:2 * D_ATTN]
    v = qkvu[:, 2 * D_ATTN:3 * D_ATTN]
    u = qkvu[:, 3 * D_ATTN:]
    return q, k, v, u


def _attention_pair(pair, q_t, kext_ref, vt_ref, table_ref, valid):
    nq = q_t.shape[1]
    row = lax.broadcasted_iota(jnp.int32, (2 * HEAD_DIM, nq), 0)
    kp = kext_ref[:, 2 * HEAD_DIM * pair:2 * HEAD_DIM * (pair + 1)]
    qp = q_t[2 * HEAD_DIM * pair:2 * HEAD_DIM * (pair + 1), :]
    outs = []
    for e in range(2):
        h = 2 * pair + e
        own = (row < HEAD_DIM) if e == 0 else (row >= HEAD_DIM)
        w = jnp.where(own, qp, 0.0).astype(BF16)
        s = _dot(kp, w)
        s = jnp.where(valid, s + table_ref[h], NEG_INF)
        m = jnp.max(s, axis=0, keepdims=True)
        p = jnp.exp2(s - m)
        l = jnp.sum(p, axis=0, keepdims=True)
        o = _dot(vt_ref[HEAD_DIM * h:HEAD_DIM * (h + 1), :], p.astype(BF16))
        outs.append(o / l)
    return outs


def _attention_t(q_t, kext_ref, vt_ref, table_ref, valid):
    outs = []
    for pair in range(N_HEADS // 2):
        outs += _attention_pair(pair, q_t, kext_ref, vt_ref, table_ref, valid)
    return jnp.concatenate(outs, axis=0)


def _pool(uext_ref, n_rows, frames_before, w_pool_ref, pool_scale):
    rows = lax.broadcasted_iota(jnp.int32, (n_rows, POOL_GROUP_DIM), 0)
    outs = []
    for g, w in enumerate(POOL_WINDOWS):
        lo = g * POOL_GROUP_DIM
        acc = uext_ref[0:HIST_ROWS + n_rows, lo:lo + POOL_GROUP_DIM]
        cur = acc[HIST_ROWS:, :]
        span = 1
        while span < w:
            acc = acc + pltpu.roll(acc, span, 0)
            span *= 2
        cnt = jnp.minimum(w, frames_before + rows + 1).astype(F32)
        diff = acc[HIST_ROWS:, :] / cnt - cur
        outs.append(_dot(diff.astype(BF16), w_pool_ref[g]))
    return jnp.concatenate(outs, axis=-1) * pool_scale


def _mix_out(x, mix, gains_ref):
    x = x + _rmsnorm(mix, gains_ref[1:2, :])
    return x, _rmsnorm(x, gains_ref[2:3, :]).astype(BF16)


def _ffn_chunk(c, hn, w_ff1_ref, w_ff2_ref):
    act = jnp.square(jnp.maximum(_dot(hn, w_ff1_ref[c]), 0.0)).astype(BF16)
    return _dot(act, w_ff2_ref[c])


def _tail(x, ff, p, gains_ref, w_ple_ref, w_gate_ref):
    x = x + _rmsnorm(ff, gains_ref[3:4, :])
    gate = jax.nn.sigmoid(_dot(x.astype(BF16), w_gate_ref[...]))
    ple = gate * _dot(p.astype(BF16), w_ple_ref[...])
    return x + _rmsnorm(ple, gains_ref[4:5, :])


def _finish(x, mixed, p, gains_ref, w_out_ref, w_ff1_ref, w_ff2_ref, w_ple_ref, w_gate_ref):
    x, hn = _mix_out(x, _dot(mixed, w_out_ref[...]), gains_ref)
    ff = None
    for c in range(D_FF // FF_CHUNK):
        part = _ffn_chunk(c, hn, w_ff1_ref, w_ff2_ref)
        ff = part if ff is None else ff + part
    return _tail(x, ff, p, gains_ref, w_ple_ref, w_gate_ref)


def _mask_lane(h):
    return HEAD_DIM if h % 2 == 0 else 0


def _prompt_tile(n_tiles, n_total, s, x_ref, p_ref, table_ref, gains_ref, pscale_ref, w_in_ref, w_pool_ref,
                 w_out_ref, w_ff1_ref, w_ff2_ref, w_ple_ref, w_gate_ref,
                 y_ref, knew_ref, vnew_ref, unew_ref,
                 kext, vt, uext, xbuf, mixbuf, x1_s, ff_s, wq_s, s_scr, p_scr):
    t = lax.rem(jnp.minimum(s, n_total - 1), n_tiles)

    @pl.when(s == 0)
    def _():
        xbuf[...] = jnp.zeros_like(xbuf)
        mixbuf[...] = jnp.zeros_like(mixbuf)
        x1_s[...] = jnp.zeros_like(x1_s)
        ff_s[...] = jnp.zeros_like(ff_s)

    @pl.when(t == 0)
    def _():
        lane = lax.broadcasted_iota(jnp.int32, (LK, 2 * HEAD_DIM), 1)
        for h in range(N_HEADS):
            kext[h] = jnp.where(lane == _mask_lane(h), NEG_INF, 0.0).astype(BF16)
        vt[...] = jnp.zeros_like(vt)
        uext[0:HIST_ROWS, :] = jnp.zeros((HIST_ROWS, D_POOL), F32)

    mix = _dot(mixbuf[...], w_out_ref[...])

    y_ref[0] = _tail(x1_s[...], ff_s[...], p_ref[0], gains_ref, w_ple_ref, w_gate_ref)

    x_prev, hn = _mix_out(xbuf[...], mix, gains_ref)
    x1_s[...] = x_prev

    def ffn_up(c):
        return jnp.square(jnp.maximum(_dot(hn, w_ff1_ref[c]), 0.0)).astype(BF16)

    def ffn_down(c, cols=slice(None)):
        return _dot(acts[c], w_ff2_ref[c, :, cols])

    q, k, v, u = _project(x_ref[0], gains_ref, w_in_ref)
    acts = {0: ffn_up(0), 1: ffn_up(1)}
    knew_ref[0] = k
    vnew_ref[0] = v
    unew_ref[0] = u[TQ - HIST_ROWS:, :]
    uext[HIST_ROWS:HIST_ROWS + TQ, :] = u
    q_t = q.T
    v_t = v.T.astype(BF16)
    row = lax.broadcasted_iota(jnp.int32, (2 * HEAD_DIM, TQ), 0)
    lane = lax.broadcasted_iota(jnp.int32, (TQ, 2 * HEAD_DIM), 1)
    ones_rows = jnp.where(lax.broadcasted_iota(jnp.int32, (V_ROWS - HEAD_DIM, TQ), 0) == 0, 1.0, 0.0).astype(BF16)
    for h in range(N_HEADS):
        lanes = slice(2 * HEAD_DIM * (h // 2), 2 * HEAD_DIM * (h // 2 + 1))
        own = (lane < HEAD_DIM) if h % 2 == 0 else (lane >= HEAD_DIM)
        kext[h, 0:LEFT, :] = kext[h, TQ:LK, :]
        kext[h, LEFT:LK, :] = jnp.where(own, k[:, lanes], 0.0).astype(BF16)
        wq_s[h] = jnp.where(row == _mask_lane(h), 1.0, q_t[lanes, :]).astype(BF16)
        vt[h, :, 0:LEFT] = vt[h, :, TQ:LK]
        vt[h, :, LEFT:LK] = jnp.concatenate([v_t[HEAD_DIM * h:HEAD_DIM * (h + 1), :], ones_rows], axis=0)

    def scores(h):
        half = LK // 2
        s_scr[h % 2, 0:half, :] = _dot(kext[h, 0:half, :], wq_s[h])
        s_scr[h % 2, half:LK, :] = _dot(kext[h, half:LK, :], wq_s[h])

    def values(h):
        o = _dot(vt[h], p_scr[h % 2])
        return o[0:HEAD_DIM, :] / o[HEAD_DIM:HEAD_DIM + 1, :]

    lo, hi = slice(0, D_MODEL // 2), slice(D_MODEL // 2, D_MODEL)
    fillers = [lambda: ("down", ffn_down(0)), lambda: ("down", ffn_down(1)), lambda: ("up", 2, ffn_up(2)),
               lambda: ("up", 3, ffn_up(3)), lambda: ("down", ffn_down(2)), lambda: ("lo", ffn_down(3, lo)),
               lambda: ("hi", ffn_down(3, hi))]
    scores(0)
    scores(1)
    ff = None
    halves = {}
    heads = []
    for h in range(N_HEADS):
        sc = s_scr[h % 2] + table_ref[h]
        p_scr[h % 2] = jnp.exp2(sc - jnp.max(sc, axis=0, keepdims=True)).astype(BF16)
        if h < len(fillers):
            kind, *rest = fillers[h]()
            if kind == "up":
                acts[rest[0]] = rest[1]
            elif kind == "down":
                ff = rest[0] if ff is None else ff + rest[0]
            else:
                halves[kind] = rest[0]
        heads.append(values(h))
        if h + 2 < N_HEADS:
            scores(h + 2)
        if h == POOL_ROUND:
            pool = _pool(uext, TQ, t * TQ, w_pool_ref, pscale_ref[...])
            uext[0:HIST_ROWS, :] = uext[TQ:TQ + HIST_ROWS, :]
            mixbuf[:, D_ATTN:] = pool.astype(BF16)
    ff_s[...] = ff + jnp.concatenate([halves["lo"], halves["hi"]], axis=-1)
    mixbuf[:, 0:D_ATTN] = jnp.concatenate(heads, axis=0).T.astype(BF16)
    xbuf[...] = x_ref[0]


def _prompt_kernel(n_tiles, n_total, x_ref, p_ref, *refs):
    n_const = 10
    consts, (y_ref, knew_ref, vnew_ref, unew_ref), scratch = refs[:n_const], refs[n_const:n_const + 4], refs[n_const + 4:]

    def tile_step(j, carry):
        rows = pl.ds(pl.multiple_of(j * TQ, TQ), TQ)
        _prompt_tile(n_tiles, n_total, TILES_PER_STEP * pl.program_id(0) + j, x_ref.at[pl.ds(j, 1)],
                     p_ref.at[pl.ds(j, 1)], *consts, y_ref.at[pl.ds(j, 1)], knew_ref.at[:, rows],
                     vnew_ref.at[:, rows], unew_ref, *scratch)
        return carry

    lax.fori_loop(0, TILES_PER_STEP, tile_step, 0)


def _resident(a):
    return pl.BlockSpec(a.shape, lambda *_: (0,) * a.ndim, pipeline_mode=pl.Buffered(1))


def _prompt_layer(x, p, table, gains, pscale, weights):
    batch, seq, _ = x.shape
    assert seq % TQ == 0 and seq >= LEFT and D_FF // FF_CHUNK == N_HEADS // 2
    assert LEFT == TILES_PER_STEP * TQ and (seq // TQ) % TILES_PER_STEP == 0
    n_tiles = seq // TQ
    n_total = batch * n_tiles
    n_blocks = n_total // TILES_PER_STEP
    mixing = lambda g: (jnp.minimum(g, n_blocks - 1), 0, 0)
    finishing = lambda g: (jnp.maximum(g - 1, 0), 0, 0)
    per_batch = lambda g: (jnp.minimum(g, n_blocks - 1) * TILES_PER_STEP // n_tiles, 0, 0)
    consts = (table, gains, pscale) + tuple(weights)
    y, k_new, v_new, u_new = pl.pallas_call(
        functools.partial(_prompt_kernel, n_tiles, n_total),
        grid=(n_blocks + 1,),
        in_specs=[pl.BlockSpec((TILES_PER_STEP, TQ, D_MODEL), mixing),
                  pl.BlockSpec((TILES_PER_STEP, TQ, D_PLE), finishing)] + [_resident(a) for a in consts],
        out_specs=[pl.BlockSpec((TILES_PER_STEP, TQ, D_MODEL), finishing), pl.BlockSpec((1, LEFT, D_ATTN), per_batch),
                   pl.BlockSpec((1, LEFT, D_ATTN), per_batch), pl.BlockSpec((1, HIST_ROWS, D_POOL), per_batch)],
        out_shape=[jax.ShapeDtypeStruct((n_total, TQ, D_MODEL), F32),
                   jax.ShapeDtypeStruct((batch, LEFT, D_ATTN), F32),
                   jax.ShapeDtypeStruct((batch, LEFT, D_ATTN), F32),
                   jax.ShapeDtypeStruct((batch, HIST_ROWS, D_POOL), F32)],
        scratch_shapes=[pltpu.VMEM((N_HEADS, LK, 2 * HEAD_DIM), BF16), pltpu.VMEM((N_HEADS, V_ROWS, LK), BF16),
                        pltpu.VMEM((HIST_ROWS + TQ, D_POOL), F32), pltpu.VMEM((TQ, D_MODEL), F32),
                        pltpu.VMEM((TQ, D_ATTN + D_POOL), BF16), pltpu.VMEM((TQ, D_MODEL), F32),
                        pltpu.VMEM((TQ, D_MODEL), F32), pltpu.VMEM((N_HEADS, 2 * HEAD_DIM, TQ), BF16),
                        pltpu.VMEM((2, LK, TQ), F32), pltpu.VMEM((2, LK, TQ), BF16)],
        compiler_params=pltpu.CompilerParams(dimension_semantics=("arbitrary",),
                                             vmem_limit_bytes=VMEM_LIMIT_BYTES),
        name="prompt_layer",
    )(x.reshape(n_total, TQ, D_MODEL), p.reshape(n_total, TQ, D_PLE), *consts)
    return y.reshape(batch, seq, D_MODEL), k_new, v_new, u_new


def _sample_kernel(n_seq, x_ref, p_ref, ck_ref, cv_ref, sp_ref, table_ref, gains_ref, pscale_ref, w_in_ref,
                   w_pool_ref, w_out_ref, w_ff1_ref, w_ff2_ref, w_ple_ref, w_gate_ref,
                   y_ref, knew_ref, vnew_ref, unew_ref, q_s, mix_s, kext, vt, uext):
    b = pl.program_id(0)
    row0 = pl.multiple_of(b * n_seq, n_seq)

    @pl.when(b == 0)
    def _():
        q, k, v, u = _project(x_ref[...], gains_ref, w_in_ref)
        q_s[...] = q
        knew_ref[...] = k
        vnew_ref[...] = v
        mix_s[:, D_ATTN:] = u
        kext[LEFT:SAMPLE_LK, :] = jnp.zeros((SAMPLE_TQ, D_ATTN), BF16)
        vt[:, LEFT:SAMPLE_LK] = jnp.zeros((D_ATTN, SAMPLE_TQ), BF16)

    pad = jnp.zeros((SAMPLE_TQ - n_seq, D_ATTN), F32)
    q = jnp.concatenate([q_s[pl.ds(row0, n_seq), :], pad], axis=0)
    v = jnp.concatenate([vnew_ref[pl.ds(row0, n_seq), :], pad], axis=0)
    u = mix_s[pl.ds(row0, n_seq), D_ATTN:]
    kext[0:LEFT, :] = ck_ref[0].astype(BF16)
    kext[LEFT:LEFT + n_seq, :] = knew_ref[pl.ds(row0, n_seq), :].astype(BF16)
    vt[:, 0:LEFT] = cv_ref[0].T.astype(BF16)
    vt[:, LEFT:SAMPLE_LK] = v.T.astype(BF16)
    uext[0:HIST_ROWS, :] = sp_ref[0]
    uext[HIST_ROWS:HIST_ROWS + n_seq, :] = u
    unew_ref[0] = u[n_seq - HIST_ROWS:, :]

    kk = lax.broadcasted_iota(jnp.int32, (SAMPLE_LK, SAMPLE_TQ), 0)
    attn = _attention_t(q.T, kext, vt, table_ref, kk < LEFT + n_seq).T
    pool = _pool(uext, n_seq, POOL_HIST, w_pool_ref, pscale_ref[...])
    mix_s[pl.ds(row0, n_seq), 0:D_ATTN] = attn[0:n_seq, :]
    mix_s[pl.ds(row0, n_seq), D_ATTN:] = pool

    @pl.when(b == pl.num_programs(0) - 1)
    def _():
        y_ref[...] = _finish(x_ref[...], mix_s[...].astype(BF16), p_ref[...], gains_ref, w_out_ref, w_ff1_ref,
                             w_ff2_ref, w_ple_ref, w_gate_ref)


def _sample_layer(x, p, cache_k, cache_v, state_pool, table, gains, pscale, weights):
    batch, n_seq, _ = x.shape
    rows = batch * n_seq
    assert cache_k.shape[1] == LEFT and HIST_ROWS <= n_seq <= SAMPLE_TQ and n_seq % 8 == 0
    full = lambda shape: pl.BlockSpec(shape, lambda b: (0,) * len(shape))
    per_b = lambda shape: pl.BlockSpec((1,) + shape, lambda b: (b, 0, 0))
    return pl.pallas_call(
        functools.partial(_sample_kernel, n_seq),
        grid=(batch,),
        in_specs=[full((rows, D_MODEL)), full((rows, D_PLE)), per_b((LEFT, D_ATTN)), per_b((LEFT, D_ATTN)),
                  per_b((HIST_ROWS, D_POOL)), full((N_HEADS, SAMPLE_LK, SAMPLE_TQ))]
        + [_resident(a) for a in (gains, pscale) + tuple(weights)],
        out_specs=[full((rows, D_MODEL)), full((rows, D_ATTN)), full((rows, D_ATTN)), per_b((HIST_ROWS, D_POOL))],
        out_shape=[jax.ShapeDtypeStruct((rows, D_MODEL), F32), jax.ShapeDtypeStruct((rows, D_ATTN), F32),
                   jax.ShapeDtypeStruct((rows, D_ATTN), F32),
                   jax.ShapeDtypeStruct((batch, HIST_ROWS, D_POOL), F32)],
        scratch_shapes=[pltpu.VMEM((rows, D_ATTN), F32), pltpu.VMEM((rows, D_ATTN + D_POOL), F32),
                        pltpu.VMEM((SAMPLE_LK, D_ATTN), BF16), pltpu.VMEM((D_ATTN, SAMPLE_LK), BF16),
                        pltpu.VMEM((HIST_ROWS + n_seq, D_POOL), F32)],
        compiler_params=pltpu.CompilerParams(dimension_semantics=("arbitrary",),
                                             vmem_limit_bytes=VMEM_LIMIT_BYTES),
        name="sample_layer",
    )(x.reshape(rows, D_MODEL), p.reshape(rows, D_PLE), cache_k, cache_v, state_pool, table, gains, pscale,
      *weights)


def kernel(x_prompt, x_sample, cache_k, cache_v, state_pool, p_prompt, p_sample, g_mix_pre, g_mix_post,
           g_ff_pre, g_ff_post, g_ple_post, w_in, rel_bias, w_pool, pool_scale, w_out, w_ff1, w_ff2, w_ple,
           w_ple_gate):
    depth = w_in.shape[0]
    batch, seq, _ = x_prompt.shape
    dec_batch, dec_seq, _ = x_sample.shape
    xp, xs = x_prompt, x_sample
    outs = [[] for _ in range(6)]
    for i in range(depth):
        table = _bias_table(rel_bias[i])
        gains = jnp.concatenate(
            [g_mix_pre[i:i + 1], g_mix_post[i:i + 1], g_ff_pre[i:i + 1], g_ff_post[i:i + 1], g_ple_post[i:i + 1],
             jnp.zeros((3, D_MODEL), F32)], axis=0)
        pscale = pool_scale[i:i + 1]
        n_chunks = D_FF // FF_CHUNK
        ff1 = w_ff1[i].astype(BF16).reshape(D_MODEL, n_chunks, FF_CHUNK).transpose(1, 0, 2)
        ff2 = w_ff2[i].astype(BF16).reshape(n_chunks, FF_CHUNK, D_MODEL)
        weights = (w_in[i].astype(BF16), w_pool[i].astype(BF16), w_out[i].astype(BF16), ff1, ff2,
                   w_ple[i].astype(BF16), w_ple_gate[i].astype(BF16))
        xp, kp, vp, up = _prompt_layer(xp, p_prompt[i], table, gains, pscale, weights)
        sp = jnp.pad(state_pool[i], ((0, 0), (HIST_ROWS - POOL_HIST, 0), (0, 0)))
        xs2, kn, vn, un = _sample_layer(xs, p_sample[i], cache_k[i].reshape(dec_batch, LEFT, D_ATTN),
                                        cache_v[i].reshape(dec_batch, LEFT, D_ATTN), sp, table, gains, pscale,
                                        weights)
        xs = xs2.reshape(dec_batch, dec_seq, D_MODEL)
        outs[0].append(kp.reshape(batch, LEFT, N_HEADS, HEAD_DIM))
        outs[1].append(vp.reshape(batch, LEFT, N_HEADS, HEAD_DIM))
        outs[2].append(up[:, HIST_ROWS - POOL_HIST:, :])
        outs[3].append(kn.reshape(dec_batch, dec_seq, N_HEADS, HEAD_DIM))
        outs[4].append(vn.reshape(dec_batch, dec_seq, N_HEADS, HEAD_DIM))
        outs[5].append(un[:, HIST_ROWS - POOL_HIST:, :])
    return (xp, xs) + tuple(jnp.stack(o) for o in outs)
```

```python
import functools

import jax
import jax.numpy as jnp
from jax import lax
from jax.experimental import pallas as pl
from jax.experimental.pallas import tpu as pltpu

D_MODEL = 1024
D_ATTN = 512
D_POOL = 512
HEAD_DIM = 64
N_HEADS = 8
CHUNK = 64
LEFT_CHUNKS = 8
LEFT = LEFT_CHUNKS * CHUNK
REL_CLIP = 128
N_REL = 2 * REL_CLIP + 1
POOL_WINDOWS = (2, 4, 8, 16)
POOL_GROUP_DIM = 128
POOL_HIST = 15
HIST_ROWS = 16
D_FF = 4096
FF_CHUNK = 1024
D_PLE = 256
EPS = 1e-6
NEG_INF = -1e30
ATTN_SCALE = HEAD_DIM ** -0.5
LOG2E = 1.4426950408889634
V_ROWS = HEAD_DIM + 16

TQ = 256
LK = LEFT + TQ
SAMPLE_TQ = 128
SAMPLE_LK = LEFT + SAMPLE_TQ
TABLE_PAD = 256
VMEM_LIMIT_BYTES = 56 * 1024 * 1024
POOL_ROUND = 2
TILES_PER_STEP = 2

F32 = jnp.float32
BF16 = jnp.bfloat16


def _bias_table_kernel(rbt_ref, out_ref, buf_a, buf_b):
    far = jnp.broadcast_to(rbt_ref[0, 0:1, :], (TABLE_PAD + LEFT - REL_CLIP, TQ))
    buf_a[0:TABLE_PAD + LEFT - REL_CLIP, :] = far
    buf_b[0:TABLE_PAD, :] = far[0:TABLE_PAD]
    buf_a[TABLE_PAD + LEFT - REL_CLIP:TABLE_PAD + LEFT + REL_CLIP, :] = jnp.broadcast_to(
        rbt_ref[0, 0:2 * REL_CLIP, :], (2 * REL_CLIP, TQ))
    buf_a[TABLE_PAD + LEFT + REL_CLIP:TABLE_PAD + LK, :] = jnp.broadcast_to(
        rbt_ref[0, 2 * REL_CLIP:2 * REL_CLIP + 1, :], (LK - LEFT - REL_CLIP, TQ))

    rows = 128
    qidx = lax.broadcasted_iota(jnp.int32, (rows, TQ), 1)
    src, dst = buf_a, buf_b
    for b in range(TQ.bit_length() - 1):
        s = 1 << b
        bit = (qidx & s) != 0
        for r0 in range(TABLE_PAD, TABLE_PAD + LK, rows):
            dst[r0:r0 + rows, :] = jnp.where(bit, src[r0 - s:r0 - s + rows, :], src[r0:r0 + rows, :])
        src, dst = dst, src

    kk = lax.broadcasted_iota(jnp.int32, (LK, TQ), 0)
    qq = lax.broadcasted_iota(jnp.int32, (LK, TQ), 1)
    d = (kk >> 6) - (qq >> 6)
    out_ref[0] = jnp.where((d >= 0) & (d <= LEFT_CHUNKS), src[TABLE_PAD:TABLE_PAD + LK, :] * LOG2E, NEG_INF)


def _bias_table(rel_bias):
    rbt = jnp.pad(rel_bias[:, ::-1], ((0, 0), (0, 384 - N_REL)))[:, :, None]
    return pl.pallas_call(
        _bias_table_kernel,
        grid=(N_HEADS,),
        in_specs=[pl.BlockSpec((1, 384, 1), lambda h: (h, 0, 0))],
        out_specs=pl.BlockSpec((1, LK, TQ), lambda h: (h, 0, 0)),
        out_shape=jax.ShapeDtypeStruct((N_HEADS, LK, TQ), F32),
        scratch_shapes=[pltpu.VMEM((TABLE_PAD + LK, TQ), F32), pltpu.VMEM((TABLE_PAD + LK, TQ), F32)],
        name="bias_table",
    )(rbt)


def _rmsnorm(x, g):
    y = x * lax.rsqrt(jnp.mean(x * x, axis=-1, keepdims=True) + EPS)
    return y * g


def _dot(a, b):
    return jnp.dot(a, b, preferred_element_type=F32)


def _project(x, gains_ref, w_in_ref):
    h = _rmsnorm(x, gains_ref[0:1, :]).astype(BF16)
    qkvu = _dot(h, w_in_ref[...])
    q = qkvu[:, 0:D_ATTN] * (ATTN_SCALE * LOG2E)
    k = qkvu[:, D_ATTN:2 * D_ATTN]
    v = qkvu[:, 2 * D_ATTN:3 * D_ATTN]
    u = qkvu[:, 3 * D_ATTN:]
    return q, k, v, u


def _attention_pair(pair, q_t, kext_ref, vt_ref, table_ref, valid):
    nq = q_t.shape[1]
    row = lax.broadcasted_iota(jnp.int32, (2 * HEAD_DIM, nq), 0)
    kp = kext_ref[:, 2 * HEAD_DIM * pair:2 * HEAD_DIM * (pair + 1)]
    qp = q_t[2 * HEAD_DIM * pair:2 * HEAD_DIM * (pair + 1), :]
    outs = []
    for e in range(2):
        h = 2 * pair + e
        own = (row < HEAD_DIM) if e == 0 else (row >= HEAD_DIM)
        w = jnp.where(own, qp, 0.0).astype(BF16)
        s = _dot(kp, w)
        s = jnp.where(valid, s + table_ref[h], NEG_INF)
        m = jnp.max(s, axis=0, keepdims=True)
        p = jnp.exp2(s - m)
        l = jnp.sum(p, axis=0, keepdims=True)
        o = _dot(vt_ref[HEAD_DIM * h:HEAD_DIM * (h + 1), :], p.astype(BF16))
        outs.append(o / l)
    return outs


def _attention_t(q_t, kext_ref, vt_ref, table_ref, valid):
    outs = []
    for pair in range(N_HEADS // 2):
        outs += _attention_pair(pair, q_t, kext_ref, vt_ref, table_ref, valid)
    return jnp.concatenate(outs, axis=0)


def _pool(uext_ref, n_rows, frames_before, w_pool_ref, pool_scale):
    rows = lax.broadcasted_iota(jnp.int32, (n_rows, POOL_GROUP_DIM), 0)
    outs = []
    for g, w in enumerate(POOL_WINDOWS):
        lo = g * POOL_GROUP_DIM
        acc = uext_ref[0:HIST_ROWS + n_rows, lo:lo + POOL_GROUP_DIM]
        cur = acc[HIST_ROWS:, :]
        span = 1
        while span < w:
            acc = acc + pltpu.roll(acc, span, 0)
            span *= 2
        cnt = jnp.minimum(w, frames_before + rows + 1).astype(F32)
        diff = acc[HIST_ROWS:, :] / cnt - cur
        outs.append(_dot(diff.astype(BF16), w_pool_ref[g]))
    return jnp.concatenate(outs, axis=-1) * pool_scale


def _mix_out(x, mix, gains_ref):
    x = x + _rmsnorm(mix, gains_ref[1:2, :])
    return x, _rmsnorm(x, gains_ref[2:3, :]).astype(BF16)


def _ffn_chunk(c, hn, w_ff1_ref, w_ff2_ref):
    chunk = slice(c * FF_CHUNK, (c + 1) * FF_CHUNK)
    act = jnp.square(jnp.maximum(_dot(hn, w_ff1_ref[:, chunk]), 0.0)).astype(BF16)
    return _dot(act, w_ff2_ref[chunk, :])


def _tail(x, ff, p, gains_ref, w_ple_ref, w_gate_ref):
    x = x + _rmsnorm(ff, gains_ref[3:4, :])
    gate = jax.nn.sigmoid(_dot(x.astype(BF16), w_gate_ref[...]))
    ple = gate * _dot(p.astype(BF16), w_ple_ref[...])
    return x + _rmsnorm(ple, gains_ref[4:5, :])


def _finish(x, mixed, p, gains_ref, w_out_ref, w_ff1_ref, w_ff2_ref, w_ple_ref, w_gate_ref):
    x, hn = _mix_out(x, _dot(mixed, w_out_ref[...]), gains_ref)
    ff = None
    for c in range(D_FF // FF_CHUNK):
        part = _ffn_chunk(c, hn, w_ff1_ref, w_ff2_ref)
        ff = part if ff is None else ff + part
    return _tail(x, ff, p, gains_ref, w_ple_ref, w_gate_ref)


def _mask_lane(h):
    return HEAD_DIM if h % 2 == 0 else 0


def _prompt_tile(n_tiles, n_total, s, x_ref, p_ref, table_ref, gains_ref, pscale_ref, w_in_ref, w_pool_ref,
                 w_out_ref, w_ff1_ref, w_ff2_ref, w_ple_ref, w_gate_ref,
                 y_ref, knew_ref, vnew_ref, unew_ref,
                 kext, vt, uext, xbuf, mixbuf, x1_s, ff_s, wq_s, s_scr, p_scr):
    t = lax.rem(jnp.minimum(s, n_total - 1), n_tiles)

    @pl.when(s == 0)
    def _():
        xbuf[...] = jnp.zeros_like(xbuf)
        mixbuf[...] = jnp.zeros_like(mixbuf)
        x1_s[...] = jnp.zeros_like(x1_s)
        ff_s[...] = jnp.zeros_like(ff_s)

    @pl.when(t == 0)
    def _():
        lane = lax.broadcasted_iota(jnp.int32, (LK, 2 * HEAD_DIM), 1)
        for h in range(N_HEADS):
            kext[h] = jnp.where(lane == _mask_lane(h), NEG_INF, 0.0).astype(BF16)
        vt[...] = jnp.zeros_like(vt)
        uext[0:HIST_ROWS, :] = jnp.zeros((HIST_ROWS, D_POOL), F32)

    mix = _dot(mixbuf[...], w_out_ref[...])

    y_ref[0] = _tail(x1_s[...], ff_s[...], p_ref[0], gains_ref, w_ple_ref, w_gate_ref)

    x_prev, hn = _mix_out(xbuf[...], mix, gains_ref)
    x1_s[...] = x_prev

    def ffn_up(c):
        return jnp.square(jnp.maximum(_dot(hn, w_ff1_ref[:, c * FF_CHUNK:(c + 1) * FF_CHUNK]), 0.0)).astype(BF16)

    def ffn_down(c, cols=slice(None)):
        return _dot(acts[c], w_ff2_ref[c * FF_CHUNK:(c + 1) * FF_CHUNK, cols])

    q, k, v, u = _project(x_ref[0], gains_ref, w_in_ref)
    acts = {0: ffn_up(0), 1: ffn_up(1)}
    knew_ref[0] = k
    vnew_ref[0] = v
    unew_ref[0] = u[TQ - HIST_ROWS:, :]
    uext[HIST_ROWS:HIST_ROWS + TQ, :] = u
    q_t = q.T
    v_t = v.T.astype(BF16)
    row = lax.broadcasted_iota(jnp.int32, (2 * HEAD_DIM, TQ), 0)
    lane = lax.broadcasted_iota(jnp.int32, (TQ, 2 * HEAD_DIM), 1)
    ones_rows = jnp.where(lax.broadcasted_iota(jnp.int32, (V_ROWS - HEAD_DIM, TQ), 0) == 0, 1.0, 0.0).astype(BF16)
    for h in range(N_HEADS):
        lanes = slice(2 * HEAD_DIM * (h // 2), 2 * HEAD_DIM * (h // 2 + 1))
        own = (lane < HEAD_DIM) if h % 2 == 0 else (lane >= HEAD_DIM)
        kext[h, 0:LEFT, :] = kext[h, TQ:LK, :]
        kext[h, LEFT:LK, :] = jnp.where(own, k[:, lanes], 0.0).astype(BF16)
        wq_s[h] = jnp.where(row == _mask_lane(h), 1.0, q_t[lanes, :]).astype(BF16)
        vt[h, :, 0:LEFT] = vt[h, :, TQ:LK]
        vt[h, :, LEFT:LK] = jnp.concatenate([v_t[HEAD_DIM * h:HEAD_DIM * (h + 1), :], ones_rows], axis=0)

    def scores(h):
        half = LK // 2
        s_scr[h % 2, 0:half, :] = _dot(kext[h, 0:half, :], wq_s[h])
        s_scr[h % 2, half:LK, :] = _dot(kext[h, half:LK, :], wq_s[h])

    def values(h):
        o = _dot(vt[h], p_scr[h % 2])
        return o[0:HEAD_DIM, :] / o[HEAD_DIM:HEAD_DIM + 1, :]

    lo, hi = slice(0, D_MODEL // 2), slice(D_MODEL // 2, D_MODEL)
    fillers = [lambda: ("down", ffn_down(0)), lambda: ("down", ffn_down(1)), lambda: ("up", 2, ffn_up(2)),
               lambda: ("up", 3, ffn_up(3)), lambda: ("down", ffn_down(2)), lambda: ("lo", ffn_down(3, lo)),
               lambda: ("hi", ffn_down(3, hi))]
    scores(0)
    scores(1)
    ff = None
    halves = {}
    heads = []
    for h in range(N_HEADS):
        sc = s_scr[h % 2] + table_ref[h]
        p_scr[h % 2] = jnp.exp2(sc - jnp.max(sc, axis=0, keepdims=True)).astype(BF16)
        if h < len(fillers):
            kind, *rest = fillers[h]()
            if kind == "up":
                acts[rest[0]] = rest[1]
            elif kind == "down":
                ff = rest[0] if ff is None else ff + rest[0]
            else:
                halves[kind] = rest[0]
        heads.append(values(h))
        if h + 2 < N_HEADS:
            scores(h + 2)
        if h == POOL_ROUND:
            pool = _pool(uext, TQ, t * TQ, w_pool_ref, pscale_ref[...])
            uext[0:HIST_ROWS, :] = uext[TQ:TQ + HIST_ROWS, :]
            mixbuf[:, D_ATTN:] = pool.astype(BF16)
    ff_s[...] = ff + jnp.concatenate([halves["lo"], halves["hi"]], axis=-1)
    mixbuf[:, 0:D_ATTN] = jnp.concatenate(heads, axis=0).T.astype(BF16)
    xbuf[...] = x_ref[0]


def _prompt_kernel(n_tiles, n_total, x_ref, p_ref, *refs):
    n_const = 10
    consts, (y_ref, knew_ref, vnew_ref, unew_ref), scratch = refs[:n_const], refs[n_const:n_const + 4], refs[n_const + 4:]

    def tile_step(j, carry):
        rows = pl.ds(pl.multiple_of(j * TQ, TQ), TQ)
        _prompt_tile(n_tiles, n_total, TILES_PER_STEP * pl.program_id(0) + j, x_ref.at[pl.ds(j, 1)],
                     p_ref.at[pl.ds(j, 1)], *consts, y_ref.at[pl.ds(j, 1)], knew_ref.at[:, rows],
                     vnew_ref.at[:, rows], unew_ref, *scratch)
        return carry

    lax.fori_loop(0, TILES_PER_STEP, tile_step, 0)


def _resident(a):
    return pl.BlockSpec(a.shape, lambda *_: (0,) * a.ndim, pipeline_mode=pl.Buffered(1))


def _prompt_layer(x, p, table, gains, pscale, weights):
    batch, seq, _ = x.shape
    assert seq % TQ == 0 and seq >= LEFT and D_FF // FF_CHUNK == N_HEADS // 2
    assert LEFT == TILES_PER_STEP * TQ and (seq // TQ) % TILES_PER_STEP == 0
    n_tiles = seq // TQ
    n_total = batch * n_tiles
    n_blocks = n_total // TILES_PER_STEP
    mixing = lambda g: (jnp.minimum(g, n_blocks - 1), 0, 0)
    finishing = lambda g: (jnp.maximum(g - 1, 0), 0, 0)
    per_batch = lambda g: (jnp.minimum(g, n_blocks - 1) * TILES_PER_STEP // n_tiles, 0, 0)
    consts = (table, gains, pscale) + tuple(weights)
    y, k_new, v_new, u_new = pl.pallas_call(
        functools.partial(_prompt_kernel, n_tiles, n_total),
        grid=(n_blocks + 1,),
        in_specs=[pl.BlockSpec((TILES_PER_STEP, TQ, D_MODEL), mixing),
                  pl.BlockSpec((TILES_PER_STEP, TQ, D_PLE), finishing)] + [_resident(a) for a in consts],
        out_specs=[pl.BlockSpec((TILES_PER_STEP, TQ, D_MODEL), finishing), pl.BlockSpec((1, LEFT, D_ATTN), per_batch),
                   pl.BlockSpec((1, LEFT, D_ATTN), per_batch), pl.BlockSpec((1, HIST_ROWS, D_POOL), per_batch)],
        out_shape=[jax.ShapeDtypeStruct((n_total, TQ, D_MODEL), F32),
                   jax.ShapeDtypeStruct((batch, LEFT, D_ATTN), F32),
                   jax.ShapeDtypeStruct((batch, LEFT, D_ATTN), F32),
                   jax.ShapeDtypeStruct((batch, HIST_ROWS, D_POOL), F32)],
        scratch_shapes=[pltpu.VMEM((N_HEADS, LK, 2 * HEAD_DIM), BF16), pltpu.VMEM((N_HEADS, V_ROWS, LK), BF16),
                        pltpu.VMEM((HIST_ROWS + TQ, D_POOL), F32), pltpu.VMEM((TQ, D_MODEL), F32),
                        pltpu.VMEM((TQ, D_ATTN + D_POOL), BF16), pltpu.VMEM((TQ, D_MODEL), F32),
                        pltpu.VMEM((TQ, D_MODEL), F32), pltpu.VMEM((N_HEADS, 2 * HEAD_DIM, TQ), BF16),
                        pltpu.VMEM((2, LK, TQ), F32), pltpu.VMEM((2, LK, TQ), BF16)],
        compiler_params=pltpu.CompilerParams(dimension_semantics=("arbitrary",),
                                             vmem_limit_bytes=VMEM_LIMIT_BYTES),
        name="prompt_layer",
    )(x.reshape(n_total, TQ, D_MODEL), p.reshape(n_total, TQ, D_PLE), *consts)
    return y.reshape(batch, seq, D_MODEL), k_new, v_new, u_new


def _sample_kernel(n_seq, x_ref, p_ref, ck_ref, cv_ref, sp_ref, table_ref, gains_ref, pscale_ref, w_in_ref,
                   w_pool_ref, w_out_ref, w_ff1_ref, w_ff2_ref, w_ple_ref, w_gate_ref,
                   y_ref, knew_ref, vnew_ref, unew_ref, q_s, mix_s, kext, vt, uext):
    b = pl.program_id(0)
    row0 = pl.multiple_of(b * n_seq, n_seq)

    @pl.when(b == 0)
    def _():
        q, k, v, u = _project(x_ref[...], gains_ref, w_in_ref)
        q_s[...] = q
        knew_ref[...] = k
        vnew_ref[...] = v
        mix_s[:, D_ATTN:] = u
        kext[LEFT:SAMPLE_LK, :] = jnp.zeros((SAMPLE_TQ, D_ATTN), BF16)
        vt[:, LEFT:SAMPLE_LK] = jnp.zeros((D_ATTN, SAMPLE_TQ), BF16)

    pad = jnp.zeros((SAMPLE_TQ - n_seq, D_ATTN), F32)
    q = jnp.concatenate([q_s[pl.ds(row0, n_seq), :], pad], axis=0)
    v = jnp.concatenate([vnew_ref[pl.ds(row0, n_seq), :], pad], axis=0)
    u = mix_s[pl.ds(row0, n_seq), D_ATTN:]
    kext[0:LEFT, :] = ck_ref[0].astype(BF16)
    kext[LEFT:LEFT + n_seq, :] = knew_ref[pl.ds(row0, n_seq), :].astype(BF16)
    vt[:, 0:LEFT] = cv_ref[0].T.astype(BF16)
    vt[:, LEFT:SAMPLE_LK] = v.T.astype(BF16)
    uext[0:HIST_ROWS, :] = sp_ref[0]
    uext[HIST_ROWS:HIST_ROWS + n_seq, :] = u
    unew_ref[0] = u[n_seq - HIST_ROWS:, :]

    kk = lax.broadcasted_iota(jnp.int32, (SAMPLE_LK, SAMPLE_TQ), 0)
    attn = _attention_t(q.T, kext, vt, table_ref, kk < LEFT + n_seq).T
    pool = _pool(uext, n_seq, POOL_HIST, w_pool_ref, pscale_ref[...])
    mix_s[pl.ds(row0, n_seq), 0:D_ATTN] = attn[0:n_seq, :]
    mix_s[pl.ds(row0, n_seq), D_ATTN:] = pool

    @pl.when(b == pl.num_programs(0) - 1)
    def _():
        y_ref[...] = _finish(x_ref[...], mix_s[...].astype(BF16), p_ref[...], gains_ref, w_out_ref, w_ff1_ref,
                             w_ff2_ref, w_ple_ref, w_gate_ref)


def _sample_layer(x, p, cache_k, cache_v, state_pool, table, gains, pscale, weights):
    batch, n_seq, _ = x.shape
    rows = batch * n_seq
    assert cache_k.shape[1] == LEFT and HIST_ROWS <= n_seq <= SAMPLE_TQ and n_seq % 8 == 0
    full = lambda shape: pl.BlockSpec(shape, lambda b: (0,) * len(shape))
    per_b = lambda shape: pl.BlockSpec((1,) + shape, lambda b: (b, 0, 0))
    return pl.pallas_call(
        functools.partial(_sample_kernel, n_seq),
        grid=(batch,),
        in_specs=[full((rows, D_MODEL)), full((rows, D_PLE)), per_b((LEFT, D_ATTN)), per_b((LEFT, D_ATTN)),
                  per_b((HIST_ROWS, D_POOL)), full((N_HEADS, SAMPLE_LK, SAMPLE_TQ))]
        + [_resident(a) for a in (gains, pscale) + tuple(weights)],
        out_specs=[full((rows, D_MODEL)), full((rows, D_ATTN)), full((rows, D_ATTN)), per_b((HIST_ROWS, D_POOL))],
        out_shape=[jax.ShapeDtypeStruct((rows, D_MODEL), F32), jax.ShapeDtypeStruct((rows, D_ATTN), F32),
                   jax.ShapeDtypeStruct((rows, D_ATTN), F32),
                   jax.ShapeDtypeStruct((batch, HIST_ROWS, D_POOL), F32)],
        scratch_shapes=[pltpu.VMEM((rows, D_ATTN), F32), pltpu.VMEM((rows, D_ATTN + D_POOL), F32),
                        pltpu.VMEM((SAMPLE_LK, D_ATTN), BF16), pltpu.VMEM((D_ATTN, SAMPLE_LK), BF16),
                        pltpu.VMEM((HIST_ROWS + n_seq, D_POOL), F32)],
        compiler_params=pltpu.CompilerParams(dimension_semantics=("arbitrary",),
                                             vmem_limit_bytes=VMEM_LIMIT_BYTES),
        name="sample_layer",
    )(x.reshape(rows, D_MODEL), p.reshape(rows, D_PLE), cache_k, cache_v, state_pool, table, gains, pscale,
      *weights)


def kernel(x_prompt, x_sample, cache_k, cache_v, state_pool, p_prompt, p_sample, g_mix_pre, g_mix_post,
           g_ff_pre, g_ff_post, g_ple_post, w_in, rel_bias, w_pool, pool_scale, w_out, w_ff1, w_ff2, w_ple,
           w_ple_gate):
    depth = w_in.shape[0]
    batch, seq, _ = x_prompt.shape
    dec_batch, dec_seq, _ = x_sample.shape
    xp, xs = x_prompt, x_sample
    outs = [[] for _ in range(6)]
    for i in range(depth):
        table = _bias_table(rel_bias[i])
        gains = jnp.concatenate(
            [g_mix_pre[i:i + 1], g_mix_post[i:i + 1], g_ff_pre[i:i + 1], g_ff_post[i:i + 1], g_ple_post[i:i + 1],
             jnp.zeros((3, D_MODEL), F32)], axis=0)
        pscale = pool_scale[i:i + 1]
        weights = tuple(w[i].astype(BF16) for w in (w_in, w_pool, w_out, w_ff1, w_ff2, w_ple, w_ple_gate))
        xp, kp, vp, up = _prompt_layer(xp, p_prompt[i], table, gains, pscale, weights)
        sp = jnp.pad(state_pool[i], ((0, 0), (HIST_ROWS - POOL_HIST, 0), (0, 0)))
        xs2, kn, vn, un = _sample_layer(xs, p_sample[i], cache_k[i].reshape(dec_batch, LEFT, D_ATTN),
                                        cache_v[i].reshape(dec_batch, LEFT, D_ATTN), sp, table, gains, pscale,
                                        weights)
        xs = xs2.reshape(dec_batch, dec_seq, D_MODEL)
        outs[0].append(kp.reshape(batch, LEFT, N_HEADS, HEAD_DIM))
        outs[1].append(vp.reshape(batch, LEFT, N_HEADS, HEAD_DIM))
        outs[2].append(up[:, HIST_ROWS - POOL_HIST:, :])
        outs[3].append(kn.reshape(dec_batch, dec_seq, N_HEADS, HEAD_DIM))
        outs[4].append(vn.reshape(dec_batch, dec_seq, N_HEADS, HEAD_DIM))
        outs[5].append(un[:, HIST_ROWS - POOL_HIST:, :])
    return (xp, xs) + tuple(jnp.stack(o) for o in outs)
```

```python
import functools

import jax
import jax.numpy as jnp
from jax import lax
from jax.experimental import pallas as pl
from jax.experimental.pallas import tpu as pltpu

D_MODEL = 1024
D_ATTN = 512
D_POOL = 512
HEAD_DIM = 64
N_HEADS = 8
CHUNK = 64
LEFT_CHUNKS = 8
LEFT = LEFT_CHUNKS * CHUNK
REL_CLIP = 128
N_REL = 2 * REL_CLIP + 1
POOL_WINDOWS = (2, 4, 8, 16)
POOL_GROUP_DIM = 128
POOL_HIST = 15
HIST_ROWS = 16
D_FF = 4096
FF_CHUNK = 1024
D_PLE = 256
EPS = 1e-6
NEG_INF = -1e30
ATTN_SCALE = HEAD_DIM ** -0.5
LOG2E = 1.4426950408889634
V_ROWS = HEAD_DIM + 16

TQ = 256
LK = LEFT + TQ
SAMPLE_TQ = 128
SAMPLE_LK = LEFT + SAMPLE_TQ
TABLE_PAD = 256
VMEM_LIMIT_BYTES = 56 * 1024 * 1024
POOL_ROUND = 2
TILES_PER_STEP = 2

F32 = jnp.float32
BF16 = jnp.bfloat16


def _bias_table_kernel(rbt_ref, out_ref, buf_a, buf_b):
    far = jnp.broadcast_to(rbt_ref[0, 0:1, :], (TABLE_PAD + LEFT - REL_CLIP, TQ))
    buf_a[0:TABLE_PAD + LEFT - REL_CLIP, :] = far
    buf_b[0:TABLE_PAD, :] = far[0:TABLE_PAD]
    buf_a[TABLE_PAD + LEFT - REL_CLIP:TABLE_PAD + LEFT + REL_CLIP, :] = jnp.broadcast_to(
        rbt_ref[0, 0:2 * REL_CLIP, :], (2 * REL_CLIP, TQ))
    buf_a[TABLE_PAD + LEFT + REL_CLIP:TABLE_PAD + LK, :] = jnp.broadcast_to(
        rbt_ref[0, 2 * REL_CLIP:2 * REL_CLIP + 1, :], (LK - LEFT - REL_CLIP, TQ))

    rows = 128
    qidx = lax.broadcasted_iota(jnp.int32, (rows, TQ), 1)
    src, dst = buf_a, buf_b
    for b in range(TQ.bit_length() - 1):
        s = 1 << b
        bit = (qidx & s) != 0
        for r0 in range(TABLE_PAD, TABLE_PAD + LK, rows):
            dst[r0:r0 + rows, :] = jnp.where(bit, src[r0 - s:r0 - s + rows, :], src[r0:r0 + rows, :])
        src, dst = dst, src

    kk = lax.broadcasted_iota(jnp.int32, (LK, TQ), 0)
    qq = lax.broadcasted_iota(jnp.int32, (LK, TQ), 1)
    d = (kk >> 6) - (qq >> 6)
    out_ref[0] = jnp.where((d >= 0) & (d <= LEFT_CHUNKS), src[TABLE_PAD:TABLE_PAD + LK, :] * LOG2E, NEG_INF)


def _bias_table(rel_bias):
    rbt = jnp.pad(rel_bias[:, ::-1], ((0, 0), (0, 384 - N_REL)))[:, :, None]
    return pl.pallas_call(
        _bias_table_kernel,
        grid=(N_HEADS,),
        in_specs=[pl.BlockSpec((1, 384, 1), lambda h: (h, 0, 0))],
        out_specs=pl.BlockSpec((1, LK, TQ), lambda h: (h, 0, 0)),
        out_shape=jax.ShapeDtypeStruct((N_HEADS, LK, TQ), F32),
        scratch_shapes=[pltpu.VMEM((TABLE_PAD + LK, TQ), F32), pltpu.VMEM((TABLE_PAD + LK, TQ), F32)],
        name="bias_table",
    )(rbt)


def _rmsnorm(x, g):
    y = x * lax.rsqrt(jnp.mean(x * x, axis=-1, keepdims=True) + EPS)
    return y * g


def _dot(a, b):
    return jnp.dot(a, b, preferred_element_type=F32)


def _project(x, gains_ref, w_in_ref):
    h = _rmsnorm(x, gains_ref[0:1, :]).astype(BF16)
    qkvu = _dot(h, w_in_ref[...])
    q = qkvu[:, 0:D_ATTN] * (ATTN_SCALE * LOG2E)
    k = qkvu[:, D_ATTN:2 * D_ATTN]
    v = qkvu[:, 2 * D_ATTN:3 * D_ATTN]
    u = qkvu[:, 3 * D_ATTN:]
    return q, k, v, u


def _attention_t(q_t, kext_ref, vt_ref, table_ref, valid):
    nq = q_t.shape[1]
    row = lax.broadcasted_iota(jnp.int32, (2 * HEAD_DIM, nq), 0)
    scores = []
    for h in range(N_HEADS):
        pair = slice(2 * HEAD_DIM * (h // 2), 2 * HEAD_DIM * (h // 2 + 1))
        own = (row < HEAD_DIM) if h % 2 == 0 else (row >= HEAD_DIM)
        scores.append(_dot(kext_ref[:, pair], jnp.where(own, q_t[pair, :], 0.0).astype(BF16)))
    probs, denoms = [], []
    for h in range(N_HEADS):
        s = jnp.where(valid, scores[h] + table_ref[h], NEG_INF)
        p = jnp.exp2(s - jnp.max(s, axis=0, keepdims=True))
        denoms.append(jnp.sum(p, axis=0, keepdims=True))
        probs.append(p.astype(BF16))
    outs = [_dot(vt_ref[HEAD_DIM * h:HEAD_DIM * (h + 1), :], probs[h]) / denoms[h] for h in range(N_HEADS)]
    return jnp.concatenate(outs, axis=0)


def _pool(uext_ref, n_rows, frames_before, w_pool_ref, pool_scale):
    rows = lax.broadcasted_iota(jnp.int32, (n_rows, POOL_GROUP_DIM), 0)
    outs = []
    for g, w in enumerate(POOL_WINDOWS):
        lo = g * POOL_GROUP_DIM
        acc = uext_ref[0:HIST_ROWS + n_rows, lo:lo + POOL_GROUP_DIM]
        cur = acc[HIST_ROWS:, :]
        span = 1
        while span < w:
            acc = acc + pltpu.roll(acc, span, 0)
            span *= 2
        cnt = jnp.minimum(w, frames_before + rows + 1).astype(F32)
        diff = acc[HIST_ROWS:, :] / cnt - cur
        outs.append(_dot(diff.astype(BF16), w_pool_ref[g]))
    return jnp.concatenate(outs, axis=-1) * pool_scale


def _mix_out(x, mix, gains_ref):
    x = x + _rmsnorm(mix, gains_ref[1:2, :])
    return x, _rmsnorm(x, gains_ref[2:3, :]).astype(BF16)


def _ffn_chunk(c, hn, w_ff1_ref, w_ff2_ref):
    chunk = slice(c * FF_CHUNK, (c + 1) * FF_CHUNK)
    act = jnp.square(jnp.maximum(_dot(hn, w_ff1_ref[:, chunk]), 0.0)).astype(BF16)
    return _dot(act, w_ff2_ref[chunk, :])


def _tail(x, ff, p, gains_ref, w_ple_ref, w_gate_ref):
    x = x + _rmsnorm(ff, gains_ref[3:4, :])
    gate = jax.nn.sigmoid(_dot(x.astype(BF16), w_gate_ref[...]))
    ple = gate * _dot(p.astype(BF16), w_ple_ref[...])
    return x + _rmsnorm(ple, gains_ref[4:5, :])


def _finish(x, mixed, p, gains_ref, w_out_ref, w_ff1_ref, w_ff2_ref, w_ple_ref, w_gate_ref):
    x, hn = _mix_out(x, _dot(mixed, w_out_ref[...]), gains_ref)
    ff = None
    for c in range(D_FF // FF_CHUNK):
        part = _ffn_chunk(c, hn, w_ff1_ref, w_ff2_ref)
        ff = part if ff is None else ff + part
    return _tail(x, ff, p, gains_ref, w_ple_ref, w_gate_ref)


def _mask_lane(h):
    return HEAD_DIM if h % 2 == 0 else 0


def _prompt_tile(n_tiles, n_total, s, x_ref, p_ref, table_ref, gains_ref, pscale_ref, w_in_ref, w_pool_ref,
                 w_out_ref, w_ff1_ref, w_ff2_ref, w_ple_ref, w_gate_ref,
                 y_ref, knew_ref, vnew_ref, unew_ref,
                 kext, vt, uext, xbuf, mixbuf, x1_s, ff_s, wq_s, s_scr, p_scr):
    t = lax.rem(jnp.minimum(s, n_total - 1), n_tiles)

    @pl.when(s == 0)
    def _():
        xbuf[...] = jnp.zeros_like(xbuf)
        mixbuf[...] = jnp.zeros_like(mixbuf)
        x1_s[...] = jnp.zeros_like(x1_s)
        ff_s[...] = jnp.zeros_like(ff_s)

    @pl.when(t == 0)
    def _():
        lane = lax.broadcasted_iota(jnp.int32, (LK, 2 * HEAD_DIM), 1)
        for h in range(N_HEADS):
            kext[h] = jnp.where(lane == _mask_lane(h), NEG_INF, 0.0).astype(BF16)
        vt[...] = jnp.zeros_like(vt)
        uext[0:HIST_ROWS, :] = jnp.zeros((HIST_ROWS, D_POOL), F32)

    mix = _dot(mixbuf[...], w_out_ref[...])

    y_ref[0] = _tail(x1_s[...], ff_s[...], p_ref[0], gains_ref, w_ple_ref, w_gate_ref)

    x_prev, hn = _mix_out(xbuf[...], mix, gains_ref)
    x1_s[...] = x_prev

    def ffn_up(c):
        return jnp.square(jnp.maximum(_dot(hn, w_ff1_ref[:, c * FF_CHUNK:(c + 1) * FF_CHUNK]), 0.0)).astype(BF16)

    def ffn_down(c, cols=slice(None)):
        return _dot(acts[c], w_ff2_ref[c * FF_CHUNK:(c + 1) * FF_CHUNK, cols])

    q, k, v, u = _project(x_ref[0], gains_ref, w_in_ref)
    acts = {0: ffn_up(0), 1: ffn_up(1)}
    knew_ref[0] = k
    vnew_ref[0] = v
    unew_ref[0] = u[TQ - HIST_ROWS:, :]
    uext[HIST_ROWS:HIST_ROWS + TQ, :] = u
    q_t = q.T
    v_t = v.T.astype(BF16)
    row = lax.broadcasted_iota(jnp.int32, (2 * HEAD_DIM, TQ), 0)
    lane = lax.broadcasted_iota(jnp.int32, (TQ, 2 * HEAD_DIM), 1)
    ones_rows = jnp.where(lax.broadcasted_iota(jnp.int32, (V_ROWS - HEAD_DIM, TQ), 0) == 0, 1.0, 0.0).astype(BF16)
    for h in range(N_HEADS):
        lanes = slice(2 * HEAD_DIM * (h // 2), 2 * HEAD_DIM * (h // 2 + 1))
        own = (lane < HEAD_DIM) if h % 2 == 0 else (lane >= HEAD_DIM)
        kext[h, 0:LEFT, :] = kext[h, TQ:LK, :]
        kext[h, LEFT:LK, :] = jnp.where(own, k[:, lanes], 0.0).astype(BF16)
        wq_s[h] = jnp.where(row == _mask_lane(h), 1.0, q_t[lanes, :]).astype(BF16)
        vt[h, :, 0:LEFT] = vt[h, :, TQ:LK]
        vt[h, :, LEFT:LK] = jnp.concatenate([v_t[HEAD_DIM * h:HEAD_DIM * (h + 1), :], ones_rows], axis=0)

    def scores(h):
        half = LK // 2
        s_scr[h % 2, 0:half, :] = _dot(kext[h, 0:half, :], wq_s[h])
        s_scr[h % 2, half:LK, :] = _dot(kext[h, half:LK, :], wq_s[h])

    def values(h):
        o = _dot(vt[h], p_scr[h % 2])
        return o[0:HEAD_DIM, :] / o[HEAD_DIM:HEAD_DIM + 1, :]

    lo, hi = slice(0, D_MODEL // 2), slice(D_MODEL // 2, D_MODEL)
    fillers = [lambda: ("down", ffn_down(0)), lambda: ("down", ffn_down(1)), lambda: ("up", 2, ffn_up(2)),
               lambda: ("up", 3, ffn_up(3)), lambda: ("down", ffn_down(2)), lambda: ("lo", ffn_down(3, lo)),
               lambda: ("hi", ffn_down(3, hi))]
    scores(0)
    scores(1)
    ff = None
    halves = {}
    heads = []
    for h in range(N_HEADS):
        sc = s_scr[h % 2] + table_ref[h]
        p_scr[h % 2] = jnp.exp2(sc - jnp.max(sc, axis=0, keepdims=True)).astype(BF16)
        if h < len(fillers):
            kind, *rest = fillers[h]()
            if kind == "up":
                acts[rest[0]] = rest[1]
            elif kind == "down":
                ff = rest[0] if ff is None else ff + rest[0]
            else:
                halves[kind] = rest[0]
        heads.append(values(h))
        if h + 2 < N_HEADS:
            scores(h + 2)
        if h == POOL_ROUND:
            pool = _pool(uext, TQ, t * TQ, w_pool_ref, pscale_ref[...])
            uext[0:HIST_ROWS, :] = uext[TQ:TQ + HIST_ROWS, :]
            mixbuf[:, D_ATTN:] = pool.astype(BF16)
    ff_s[...] = ff + jnp.concatenate([halves["lo"], halves["hi"]], axis=-1)
    mixbuf[:, 0:D_ATTN] = jnp.concatenate(heads, axis=0).T.astype(BF16)
    xbuf[...] = x_ref[0]


def _prompt_kernel(n_tiles, n_total, x_ref, p_ref, *refs):
    n_const = 10
    consts, (y_ref, knew_ref, vnew_ref, unew_ref), scratch = refs[:n_const], refs[n_const:n_const + 4], refs[n_const + 4:]

    def tile_step(j, carry):
        rows = pl.ds(pl.multiple_of(j * TQ, TQ), TQ)
        _prompt_tile(n_tiles, n_total, TILES_PER_STEP * pl.program_id(0) + j, x_ref.at[pl.ds(j, 1)],
                     p_ref.at[pl.ds(j, 1)], *consts, y_ref.at[pl.ds(j, 1)], knew_ref.at[:, rows],
                     vnew_ref.at[:, rows], unew_ref, *scratch)
        return carry

    lax.fori_loop(0, TILES_PER_STEP, tile_step, 0)


def _resident(a):
    return pl.BlockSpec(a.shape, lambda *_: (0,) * a.ndim, pipeline_mode=pl.Buffered(1))


def _prompt_layer(x, p, table, gains, pscale, weights):
    batch, seq, _ = x.shape
    assert seq % TQ == 0 and seq >= LEFT and D_FF // FF_CHUNK == N_HEADS // 2
    assert LEFT == TILES_PER_STEP * TQ and (seq // TQ) % TILES_PER_STEP == 0
    n_tiles = seq // TQ
    n_total = batch * n_tiles
    n_blocks = n_total // TILES_PER_STEP
    mixing = lambda g: (jnp.minimum(g, n_blocks - 1), 0, 0)
    finishing = lambda g: (jnp.maximum(g - 1, 0), 0, 0)
    per_batch = lambda g: (jnp.minimum(g, n_blocks - 1) * TILES_PER_STEP // n_tiles, 0, 0)
    consts = (table, gains, pscale) + tuple(weights)
    y, k_new, v_new, u_new = pl.pallas_call(
        functools.partial(_prompt_kernel, n_tiles, n_total),
        grid=(n_blocks + 1,),
        in_specs=[pl.BlockSpec((TILES_PER_STEP, TQ, D_MODEL), mixing),
                  pl.BlockSpec((TILES_PER_STEP, TQ, D_PLE), finishing)] + [_resident(a) for a in consts],
        out_specs=[pl.BlockSpec((TILES_PER_STEP, TQ, D_MODEL), finishing), pl.BlockSpec((1, LEFT, D_ATTN), per_batch),
                   pl.BlockSpec((1, LEFT, D_ATTN), per_batch), pl.BlockSpec((1, HIST_ROWS, D_POOL), per_batch)],
        out_shape=[jax.ShapeDtypeStruct((n_total, TQ, D_MODEL), F32),
                   jax.ShapeDtypeStruct((batch, LEFT, D_ATTN), F32),
                   jax.ShapeDtypeStruct((batch, LEFT, D_ATTN), F32),
                   jax.ShapeDtypeStruct((batch, HIST_ROWS, D_POOL), F32)],
        scratch_shapes=[pltpu.VMEM((N_HEADS, LK, 2 * HEAD_DIM), BF16), pltpu.VMEM((N_HEADS, V_ROWS, LK), BF16),
                        pltpu.VMEM((HIST_ROWS + TQ, D_POOL), F32), pltpu.VMEM((TQ, D_MODEL), F32),
                        pltpu.VMEM((TQ, D_ATTN + D_POOL), BF16), pltpu.VMEM((TQ, D_MODEL), F32),
                        pltpu.VMEM((TQ, D_MODEL), F32), pltpu.VMEM((N_HEADS, 2 * HEAD_DIM, TQ), BF16),
                        pltpu.VMEM((2, LK, TQ), F32), pltpu.VMEM((2, LK, TQ), BF16)],
        compiler_params=pltpu.CompilerParams(dimension_semantics=("arbitrary",),
                                             vmem_limit_bytes=VMEM_LIMIT_BYTES),
        name="prompt_layer",
    )(x.reshape(n_total, TQ, D_MODEL), p.reshape(n_total, TQ, D_PLE), *consts)
    return y.reshape(batch, seq, D_MODEL), k_new, v_new, u_new


def _sample_kernel(n_seq, x_ref, p_ref, ck_ref, cv_ref, sp_ref, table_ref, gains_ref, pscale_ref, w_in_ref,
                   w_pool_ref, w_out_ref, w_ff1_ref, w_ff2_ref, w_ple_ref, w_gate_ref,
                   y_ref, knew_ref, vnew_ref, unew_ref, q_s, mix_s, kext, vt, uext):
    b = pl.program_id(0)
    row0 = pl.multiple_of(b * n_seq, n_seq)

    @pl.when(b == 0)
    def _():
        q, k, v, u = _project(x_ref[...], gains_ref, w_in_ref)
        q_s[...] = q
        knew_ref[...] = k
        vnew_ref[...] = v
        mix_s[:, D_ATTN:] = u
        kext[LEFT:SAMPLE_LK, :] = jnp.zeros((SAMPLE_TQ, D_ATTN), BF16)
        vt[:, LEFT:SAMPLE_LK] = jnp.zeros((D_ATTN, SAMPLE_TQ), BF16)

    pad = jnp.zeros((SAMPLE_TQ - n_seq, D_ATTN), F32)
    q = jnp.concatenate([q_s[pl.ds(row0, n_seq), :], pad], axis=0)
    v = jnp.concatenate([vnew_ref[pl.ds(row0, n_seq), :], pad], axis=0)
    u = mix_s[pl.ds(row0, n_seq), D_ATTN:]
    kext[0:LEFT, :] = ck_ref[0].astype(BF16)
    kext[LEFT:LEFT + n_seq, :] = knew_ref[pl.ds(row0, n_seq), :].astype(BF16)
    vt[:, 0:LEFT] = cv_ref[0].T.astype(BF16)
    vt[:, LEFT:SAMPLE_LK] = v.T.astype(BF16)
    uext[0:HIST_ROWS, :] = sp_ref[0]
    uext[HIST_ROWS:HIST_ROWS + n_seq, :] = u
    unew_ref[0] = u[n_seq - HIST_ROWS:, :]

    kk = lax.broadcasted_iota(jnp.int32, (SAMPLE_LK, SAMPLE_TQ), 0)
    attn = _attention_t(q.T, kext, vt, table_ref, kk < LEFT + n_seq).T
    pool = _pool(uext, n_seq, POOL_HIST, w_pool_ref, pscale_ref[...])
    mix_s[pl.ds(row0, n_seq), 0:D_ATTN] = attn[0:n_seq, :]
    mix_s[pl.ds(row0, n_seq), D_ATTN:] = pool

    @pl.when(b == pl.num_programs(0) - 1)
    def _():
        y_ref[...] = _finish(x_ref[...], mix_s[...].astype(BF16), p_ref[...], gains_ref, w_out_ref, w_ff1_ref,
                             w_ff2_ref, w_ple_ref, w_gate_ref)


def _sample_layer(x, p, cache_k, cache_v, state_pool, table, gains, pscale, weights):
    batch, n_seq, _ = x.shape
    rows = batch * n_seq
    assert cache_k.shape[1] == LEFT and HIST_ROWS <= n_seq <= SAMPLE_TQ and n_seq % 8 == 0
    full = lambda shape: pl.BlockSpec(shape, lambda b: (0,) * len(shape))
    per_b = lambda shape: pl.BlockSpec((1,) + shape, lambda b: (b, 0, 0))
    return pl.pallas_call(
        functools.partial(_sample_kernel, n_seq),
        grid=(batch,),
        in_specs=[full((rows, D_MODEL)), full((rows, D_PLE)), per_b((LEFT, D_ATTN)), per_b((LEFT, D_ATTN)),
                  per_b((HIST_ROWS, D_POOL)), full((N_HEADS, SAMPLE_LK, SAMPLE_TQ))]
        + [_resident(a) for a in (gains, pscale) + tuple(weights)],
        out_specs=[full((rows, D_MODEL)), full((rows, D_ATTN)), full((rows, D_ATTN)), per_b((HIST_ROWS, D_POOL))],
        out_shape=[jax.ShapeDtypeStruct((rows, D_MODEL), F32), jax.ShapeDtypeStruct((rows, D_ATTN), F32),
                   jax.ShapeDtypeStruct((rows, D_ATTN), F32),
                   jax.ShapeDtypeStruct((batch, HIST_ROWS, D_POOL), F32)],
        scratch_shapes=[pltpu.VMEM((rows, D_ATTN), F32), pltpu.VMEM((rows, D_ATTN + D_POOL), F32),
                        pltpu.VMEM((SAMPLE_LK, D_ATTN), BF16), pltpu.VMEM((D_ATTN, SAMPLE_LK), BF16),
                        pltpu.VMEM((HIST_ROWS + n_seq, D_POOL), F32)],
        compiler_params=pltpu.CompilerParams(dimension_semantics=("arbitrary",),
                                             vmem_limit_bytes=VMEM_LIMIT_BYTES),
        name="sample_layer",
    )(x.reshape(rows, D_MODEL), p.reshape(rows, D_PLE), cache_k, cache_v, state_pool, table, gains, pscale,
      *weights)


def kernel(x_prompt, x_sample, cache_k, cache_v, state_pool, p_prompt, p_sample, g_mix_pre, g_mix_post,
           g_ff_pre, g_ff_post, g_ple_post, w_in, rel_bias, w_pool, pool_scale, w_out, w_ff1, w_ff2, w_ple,
           w_ple_gate):
    depth = w_in.shape[0]
    batch, seq, _ = x_prompt.shape
    dec_batch, dec_seq, _ = x_sample.shape
    xp, xs = x_prompt, x_sample
    outs = [[] for _ in range(6)]
    for i in range(depth):
        table = _bias_table(rel_bias[i])
        gains = jnp.concatenate(
            [g_mix_pre[i:i + 1], g_mix_post[i:i + 1], g_ff_pre[i:i + 1], g_ff_post[i:i + 1], g_ple_post[i:i + 1],
             jnp.zeros((3, D_MODEL), F32)], axis=0)
        pscale = pool_scale[i:i + 1]
        weights = tuple(w[i].astype(BF16) for w in (w_in, w_pool, w_out, w_ff1, w_ff2, w_ple, w_ple_gate))
        xp, kp, vp, up = _prompt_layer(xp, p_prompt[i], table, gains, pscale, weights)
        sp = jnp.pad(state_pool[i], ((0, 0), (HIST_ROWS - POOL_HIST, 0), (0, 0)))
        xs2, kn, vn, un = _sample_layer(xs, p_sample[i], cache_k[i].reshape(dec_batch, LEFT, D_ATTN),
                                        cache_v[i].reshape(dec_batch, LEFT, D_ATTN), sp, table, gains, pscale,
                                        weights)
        xs = xs2.reshape(dec_batch, dec_seq, D_MODEL)
        outs[0].append(kp.reshape(batch, LEFT, N_HEADS, HEAD_DIM))
        outs[1].append(vp.reshape(batch, LEFT, N_HEADS, HEAD_DIM))
        outs[2].append(up[:, HIST_ROWS - POOL_HIST:, :])
        outs[3].append(kn.reshape(dec_batch, dec_seq, N_HEADS, HEAD_DIM))
        outs[4].append(vn.reshape(dec_batch, dec_seq, N_HEADS, HEAD_DIM))
        outs[5].append(un[:, HIST_ROWS - POOL_HIST:, :])
    return (xp, xs) + tuple(jnp.stack(o) for o in outs)
```

```python
import functools

import jax
import jax.numpy as jnp
from jax import lax
from jax.experimental import pallas as pl
from jax.experimental.pallas import tpu as pltpu

D_MODEL = 1024
D_ATTN = 512
D_POOL = 512
HEAD_DIM = 64
N_HEADS = 8
CHUNK = 64
LEFT_CHUNKS = 8
LEFT = LEFT_CHUNKS * CHUNK
REL_CLIP = 128
N_REL = 2 * REL_CLIP + 1
POOL_WINDOWS = (2, 4, 8, 16)
POOL_GROUP_DIM = 128
POOL_HIST = 15
HIST_ROWS = 16
D_FF = 4096
FF_CHUNK = 1024
D_PLE = 256
EPS = 1e-6
NEG_INF = -1e30
ATTN_SCALE = HEAD_DIM ** -0.5
LOG2E = 1.4426950408889634
V_ROWS = HEAD_DIM + 16

TQ = 256
LK = LEFT + TQ
SAMPLE_TQ = 128
SAMPLE_LK = LEFT + SAMPLE_TQ
TABLE_PAD = 256
VMEM_LIMIT_BYTES = 56 * 1024 * 1024
POOL_ROUND = 2
TILES_PER_STEP = 2

F32 = jnp.float32
BF16 = jnp.bfloat16


def _bias_table_kernel(rbt_ref, out_ref, buf_a, buf_b):
    far = jnp.broadcast_to(rbt_ref[0, 0:1, :], (TABLE_PAD + LEFT - REL_CLIP, TQ))
    buf_a[0:TABLE_PAD + LEFT - REL_CLIP, :] = far
    buf_b[0:TABLE_PAD, :] = far[0:TABLE_PAD]
    buf_a[TABLE_PAD + LEFT - REL_CLIP:TABLE_PAD + LEFT + REL_CLIP, :] = jnp.broadcast_to(
        rbt_ref[0, 0:2 * REL_CLIP, :], (2 * REL_CLIP, TQ))
    buf_a[TABLE_PAD + LEFT + REL_CLIP:TABLE_PAD + LK, :] = jnp.broadcast_to(
        rbt_ref[0, 2 * REL_CLIP:2 * REL_CLIP + 1, :], (LK - LEFT - REL_CLIP, TQ))

    rows = 128
    qidx = lax.broadcasted_iota(jnp.int32, (rows, TQ), 1)
    src, dst = buf_a, buf_b
    for b in range(TQ.bit_length() - 1):
        s = 1 << b
        bit = (qidx & s) != 0
        for r0 in range(TABLE_PAD, TABLE_PAD + LK, rows):
            dst[r0:r0 + rows, :] = jnp.where(bit, src[r0 - s:r0 - s + rows, :], src[r0:r0 + rows, :])
        src, dst = dst, src

    kk = lax.broadcasted_iota(jnp.int32, (LK, TQ), 0)
    qq = lax.broadcasted_iota(jnp.int32, (LK, TQ), 1)
    d = (kk >> 6) - (qq >> 6)
    out_ref[0] = jnp.where((d >= 0) & (d <= LEFT_CHUNKS), src[TABLE_PAD:TABLE_PAD + LK, :] * LOG2E, NEG_INF)


def _bias_table(rel_bias):
    rbt = jnp.pad(rel_bias[:, ::-1], ((0, 0), (0, 384 - N_REL)))[:, :, None]
    return pl.pallas_call(
        _bias_table_kernel,
        grid=(N_HEADS,),
        in_specs=[pl.BlockSpec((1, 384, 1), lambda h: (h, 0, 0))],
        out_specs=pl.BlockSpec((1, LK, TQ), lambda h: (h, 0, 0)),
        out_shape=jax.ShapeDtypeStruct((N_HEADS, LK, TQ), F32),
        scratch_shapes=[pltpu.VMEM((TABLE_PAD + LK, TQ), F32), pltpu.VMEM((TABLE_PAD + LK, TQ), F32)],
        name="bias_table",
    )(rbt)


def _rmsnorm(x, g):
    y = x * lax.rsqrt(jnp.mean(x * x, axis=-1, keepdims=True) + EPS)
    return y * g


def _dot(a, b):
    return jnp.dot(a, b, preferred_element_type=F32)


def _pack_rows(w):
    *lead, k, n = w.shape
    pairs = jnp.moveaxis(w.reshape(*lead, k // 2, 2, n), -2, -1)
    return lax.bitcast_convert_type(pairs, jnp.uint32)


def _unpack_rows(w):
    return pltpu.bitcast(w, BF16)


def _project(x, gains_ref, w_in_ref):
    h = _rmsnorm(x, gains_ref[0:1, :]).astype(BF16)
    qkvu = _dot(h, _unpack_rows(w_in_ref[...]))
    q = qkvu[:, 0:D_ATTN] * (ATTN_SCALE * LOG2E)
    k = qkvu[:, D_ATTN:2 * D_ATTN]
    v = qkvu[:, 2 * D_ATTN:3 * D_ATTN]
    u = qkvu[:, 3 * D_ATTN:]
    return q, k, v, u


def _attention_t(q_t, kext_ref, vt_ref, table_ref, valid):
    nq = q_t.shape[1]
    row = lax.broadcasted_iota(jnp.int32, (2 * HEAD_DIM, nq), 0)
    scores = []
    for h in range(N_HEADS):
        pair = slice(2 * HEAD_DIM * (h // 2), 2 * HEAD_DIM * (h // 2 + 1))
        own = (row < HEAD_DIM) if h % 2 == 0 else (row >= HEAD_DIM)
        scores.append(_dot(kext_ref[:, pair], jnp.where(own, q_t[pair, :], 0.0).astype(BF16)))
    probs, denoms = [], []
    for h in range(N_HEADS):
        s = jnp.where(valid, scores[h] + table_ref[h], NEG_INF)
        p = jnp.exp2(s - jnp.max(s, axis=0, keepdims=True))
        denoms.append(jnp.sum(p, axis=0, keepdims=True))
        probs.append(p.astype(BF16))
    outs = [_dot(vt_ref[HEAD_DIM * h:HEAD_DIM * (h + 1), :], probs[h]) / denoms[h] for h in range(N_HEADS)]
    return jnp.concatenate(outs, axis=0)


def _pool(uext_ref, n_rows, frames_before, w_pool_ref, pool_scale):
    rows = lax.broadcasted_iota(jnp.int32, (n_rows, POOL_GROUP_DIM), 0)
    outs = []
    for g, w in enumerate(POOL_WINDOWS):
        lo = g * POOL_GROUP_DIM
        acc = uext_ref[0:HIST_ROWS + n_rows, lo:lo + POOL_GROUP_DIM]
        cur = acc[HIST_ROWS:, :]
        span = 1
        while span < w:
            acc = acc + pltpu.roll(acc, span, 0)
            span *= 2
        cnt = jnp.minimum(w, frames_before + rows + 1).astype(F32)
        diff = acc[HIST_ROWS:, :] / cnt - cur
        outs.append(_dot(diff.astype(BF16), _unpack_rows(w_pool_ref[g])))
    return jnp.concatenate(outs, axis=-1) * pool_scale


def _mix_out(x, mix, gains_ref):
    x = x + _rmsnorm(mix, gains_ref[1:2, :])
    return x, _rmsnorm(x, gains_ref[2:3, :]).astype(BF16)


def _ffn_chunk(c, hn, w_ff1_ref, w_ff2_ref):
    act = jnp.square(jnp.maximum(_dot(hn, _unpack_rows(w_ff1_ref[:, c * FF_CHUNK:(c + 1) * FF_CHUNK])), 0.0))
    return _dot(act.astype(BF16), _unpack_rows(w_ff2_ref[c * FF_CHUNK // 2:(c + 1) * FF_CHUNK // 2, :]))


def _tail(x, ff, p, gains_ref, w_ple_ref, w_gate_ref):
    x = x + _rmsnorm(ff, gains_ref[3:4, :])
    gate = jax.nn.sigmoid(_dot(x.astype(BF16), _unpack_rows(w_gate_ref[...])))
    ple = gate * _dot(p.astype(BF16), _unpack_rows(w_ple_ref[...]))
    return x + _rmsnorm(ple, gains_ref[4:5, :])


def _finish(x, mixed, p, gains_ref, w_out_ref, w_ff1_ref, w_ff2_ref, w_ple_ref, w_gate_ref):
    x, hn = _mix_out(x, _dot(mixed, _unpack_rows(w_out_ref[...])), gains_ref)
    ff = None
    for c in range(D_FF // FF_CHUNK):
        part = _ffn_chunk(c, hn, w_ff1_ref, w_ff2_ref)
        ff = part if ff is None else ff + part
    return _tail(x, ff, p, gains_ref, w_ple_ref, w_gate_ref)


def _mask_lane(h):
    return HEAD_DIM if h % 2 == 0 else 0


def _prompt_tile(n_tiles, n_total, s, x_ref, p_ref, table_ref, gains_ref, pscale_ref, w_in_ref, w_pool_ref,
                 w_out_ref, w_ff1_ref, w_ff2_ref, w_ple_ref, w_gate_ref,
                 y_ref, knew_ref, vnew_ref, unew_ref,
                 kext, vt, uext, xbuf, mixbuf, x1_s, ff_s, wq_s, s_scr, p_scr):
    t = lax.rem(jnp.minimum(s, n_total - 1), n_tiles)

    @pl.when(s == 0)
    def _():
        xbuf[...] = jnp.zeros_like(xbuf)
        mixbuf[...] = jnp.zeros_like(mixbuf)
        x1_s[...] = jnp.zeros_like(x1_s)
        ff_s[...] = jnp.zeros_like(ff_s)

    @pl.when(t == 0)
    def _():
        lane = lax.broadcasted_iota(jnp.int32, (LK, 2 * HEAD_DIM), 1)
        for h in range(N_HEADS):
            kext[h] = jnp.where(lane == _mask_lane(h), NEG_INF, 0.0).astype(BF16)
        vt[...] = jnp.zeros_like(vt)
        uext[0:HIST_ROWS, :] = jnp.zeros((HIST_ROWS, D_POOL), F32)

    mix = _dot(mixbuf[...], _unpack_rows(w_out_ref[...]))

    y_ref[0] = _tail(x1_s[...], ff_s[...], p_ref[0], gains_ref, w_ple_ref, w_gate_ref)

    x_prev, hn = _mix_out(xbuf[...], mix, gains_ref)
    x1_s[...] = x_prev

    def ffn_up(c):
        w = _unpack_rows(w_ff1_ref[:, c * FF_CHUNK:(c + 1) * FF_CHUNK])
        return jnp.square(jnp.maximum(_dot(hn, w), 0.0)).astype(BF16)

    def ffn_down(c, cols=slice(None)):
        return _dot(acts[c], _unpack_rows(w_ff2_ref[c * FF_CHUNK // 2:(c + 1) * FF_CHUNK // 2, cols]))

    q, k, v, u = _project(x_ref[0], gains_ref, w_in_ref)
    acts = {0: ffn_up(0), 1: ffn_up(1)}
    knew_ref[0] = k
    vnew_ref[0] = v
    unew_ref[0] = u[TQ - HIST_ROWS:, :]
    uext[HIST_ROWS:HIST_ROWS + TQ, :] = u
    q_t = q.T
    v_t = v.T.astype(BF16)
    row = lax.broadcasted_iota(jnp.int32, (2 * HEAD_DIM, TQ), 0)
    lane = lax.broadcasted_iota(jnp.int32, (TQ, 2 * HEAD_DIM), 1)
    ones_rows = jnp.where(lax.broadcasted_iota(jnp.int32, (V_ROWS - HEAD_DIM, TQ), 0) == 0, 1.0, 0.0).astype(BF16)
    for h in range(N_HEADS):
        lanes = slice(2 * HEAD_DIM * (h // 2), 2 * HEAD_DIM * (h // 2 + 1))
        own = (lane < HEAD_DIM) if h % 2 == 0 else (lane >= HEAD_DIM)
        kext[h, 0:LEFT, :] = kext[h, TQ:LK, :]
        kext[h, LEFT:LK, :] = jnp.where(own, k[:, lanes], 0.0).astype(BF16)
        wq_s[h] = jnp.where(row == _mask_lane(h), 1.0, q_t[lanes, :]).astype(BF16)
        vt[h, :, 0:LEFT] = vt[h, :, TQ:LK]
        vt[h, :, LEFT:LK] = jnp.concatenate([v_t[HEAD_DIM * h:HEAD_DIM * (h + 1), :], ones_rows], axis=0)

    def scores(h):
        half = LK // 2
        s_scr[h % 2, 0:half, :] = _dot(kext[h, 0:half, :], wq_s[h])
        s_scr[h % 2, half:LK, :] = _dot(kext[h, half:LK, :], wq_s[h])

    def values(h):
        o = _dot(vt[h], p_scr[h % 2])
        return o[0:HEAD_DIM, :] / o[HEAD_DIM:HEAD_DIM + 1, :]

    lo, hi = slice(0, D_MODEL // 2), slice(D_MODEL // 2, D_MODEL)
    fillers = [lambda: ("down", ffn_down(0)), lambda: ("down", ffn_down(1)), lambda: ("up", 2, ffn_up(2)),
               lambda: ("up", 3, ffn_up(3)), lambda: ("down", ffn_down(2)), lambda: ("lo", ffn_down(3, lo)),
               lambda: ("hi", ffn_down(3, hi))]
    scores(0)
    scores(1)
    ff = None
    halves = {}
    heads = []
    for h in range(N_HEADS):
        sc = s_scr[h % 2] + table_ref[h]
        p_scr[h % 2] = jnp.exp2(sc - jnp.max(sc, axis=0, keepdims=True)).astype(BF16)
        if h < len(fillers):
            kind, *rest = fillers[h]()
            if kind == "up":
                acts[rest[0]] = rest[1]
            elif kind == "down":
                ff = rest[0] if ff is None else ff + rest[0]
            else:
                halves[kind] = rest[0]
        heads.append(values(h))
        if h + 2 < N_HEADS:
            scores(h + 2)
        if h == POOL_ROUND:
            pool = _pool(uext, TQ, t * TQ, w_pool_ref, pscale_ref[...])
            uext[0:HIST_ROWS, :] = uext[TQ:TQ + HIST_ROWS, :]
            mixbuf[:, D_ATTN:] = pool.astype(BF16)
    ff_s[...] = ff + jnp.concatenate([halves["lo"], halves["hi"]], axis=-1)
    mixbuf[:, 0:D_ATTN] = jnp.concatenate(heads, axis=0).T.astype(BF16)
    xbuf[...] = x_ref[0]


def _prompt_kernel(n_tiles, n_total, x_ref, p_ref, *refs):
    n_const = 10
    consts, (y_ref, knew_ref, vnew_ref, unew_ref), scratch = refs[:n_const], refs[n_const:n_const + 4], refs[n_const + 4:]

    def tile_step(j, carry):
        rows = pl.ds(pl.multiple_of(j * TQ, TQ), TQ)
        _prompt_tile(n_tiles, n_total, TILES_PER_STEP * pl.program_id(0) + j, x_ref.at[pl.ds(j, 1)],
                     p_ref.at[pl.ds(j, 1)], *consts, y_ref.at[pl.ds(j, 1)], knew_ref.at[:, rows],
                     vnew_ref.at[:, rows], unew_ref, *scratch)
        return carry

    lax.fori_loop(0, TILES_PER_STEP, tile_step, 0)


def _resident(a):
    return pl.BlockSpec(a.shape, lambda *_: (0,) * a.ndim, pipeline_mode=pl.Buffered(1))


def _prompt_layer(x, p, table, gains, pscale, weights):
    batch, seq, _ = x.shape
    assert seq % TQ == 0 and seq >= LEFT and D_FF // FF_CHUNK == N_HEADS // 2
    assert LEFT == TILES_PER_STEP * TQ and (seq // TQ) % TILES_PER_STEP == 0
    n_tiles = seq // TQ
    n_total = batch * n_tiles
    n_blocks = n_total // TILES_PER_STEP
    mixing = lambda g: (jnp.minimum(g, n_blocks - 1), 0, 0)
    finishing = lambda g: (jnp.maximum(g - 1, 0), 0, 0)
    per_batch = lambda g: (jnp.minimum(g, n_blocks - 1) * TILES_PER_STEP // n_tiles, 0, 0)
    consts = (table, gains, pscale) + tuple(weights)
    y, k_new, v_new, u_new = pl.pallas_call(
        functools.partial(_prompt_kernel, n_tiles, n_total),
        grid=(n_blocks + 1,),
        in_specs=[pl.BlockSpec((TILES_PER_STEP, TQ, D_MODEL), mixing),
                  pl.BlockSpec((TILES_PER_STEP, TQ, D_PLE), finishing)] + [_resident(a) for a in consts],
        out_specs=[pl.BlockSpec((TILES_PER_STEP, TQ, D_MODEL), finishing), pl.BlockSpec((1, LEFT, D_ATTN), per_batch),
                   pl.BlockSpec((1, LEFT, D_ATTN), per_batch), pl.BlockSpec((1, HIST_ROWS, D_POOL), per_batch)],
        out_shape=[jax.ShapeDtypeStruct((n_total, TQ, D_MODEL), F32),
                   jax.ShapeDtypeStruct((batch, LEFT, D_ATTN), F32),
                   jax.ShapeDtypeStruct((batch, LEFT, D_ATTN), F32),
                   jax.ShapeDtypeStruct((batch, HIST_ROWS, D_POOL), F32)],
        scratch_shapes=[pltpu.VMEM((N_HEADS, LK, 2 * HEAD_DIM), BF16), pltpu.VMEM((N_HEADS, V_ROWS, LK), BF16),
                        pltpu.VMEM((HIST_ROWS + TQ, D_POOL), F32), pltpu.VMEM((TQ, D_MODEL), F32),
                        pltpu.VMEM((TQ, D_ATTN + D_POOL), BF16), pltpu.VMEM((TQ, D_MODEL), F32),
                        pltpu.VMEM((TQ, D_MODEL), F32), pltpu.VMEM((N_HEADS, 2 * HEAD_DIM, TQ), BF16),
                        pltpu.VMEM((2, LK, TQ), F32), pltpu.VMEM((2, LK, TQ), BF16)],
        compiler_params=pltpu.CompilerParams(dimension_semantics=("arbitrary",),
                                             vmem_limit_bytes=VMEM_LIMIT_BYTES),
        name="prompt_layer",
    )(x.reshape(n_total, TQ, D_MODEL), p.reshape(n_total, TQ, D_PLE), *consts)
    return y.reshape(batch, seq, D_MODEL), k_new, v_new, u_new


def _sample_kernel(n_seq, x_ref, p_ref, ck_ref, cv_ref, sp_ref, table_ref, gains_ref, pscale_ref, w_in_ref,
                   w_pool_ref, w_out_ref, w_ff1_ref, w_ff2_ref, w_ple_ref, w_gate_ref,
                   y_ref, knew_ref, vnew_ref, unew_ref, q_s, mix_s, kext, vt, uext):
    b = pl.program_id(0)
    row0 = pl.multiple_of(b * n_seq, n_seq)

    @pl.when(b == 0)
    def _():
        q, k, v, u = _project(x_ref[...], gains_ref, w_in_ref)
        q_s[...] = q
        knew_ref[...] = k
        vnew_ref[...] = v
        mix_s[:, D_ATTN:] = u
        kext[LEFT:SAMPLE_LK, :] = jnp.zeros((SAMPLE_TQ, D_ATTN), BF16)
        vt[:, LEFT:SAMPLE_LK] = jnp.zeros((D_ATTN, SAMPLE_TQ), BF16)

    pad = jnp.zeros((SAMPLE_TQ - n_seq, D_ATTN), F32)
    q = jnp.concatenate([q_s[pl.ds(row0, n_seq), :], pad], axis=0)
    v = jnp.concatenate([vnew_ref[pl.ds(row0, n_seq), :], pad], axis=0)
    u = mix_s[pl.ds(row0, n_seq), D_ATTN:]
    kext[0:LEFT, :] = ck_ref[0].astype(BF16)
    kext[LEFT:LEFT + n_seq, :] = knew_ref[pl.ds(row0, n_seq), :].astype(BF16)
    vt[:, 0:LEFT] = cv_ref[0].T.astype(BF16)
    vt[:, LEFT:SAMPLE_LK] = v.T.astype(BF16)
    uext[0:HIST_ROWS, :] = sp_ref[0]
    uext[HIST_ROWS:HIST_ROWS + n_seq, :] = u
    unew_ref[0] = u[n_seq - HIST_ROWS:, :]

    kk = lax.broadcasted_iota(jnp.int32, (SAMPLE_LK, SAMPLE_TQ), 0)
    attn = _attention_t(q.T, kext, vt, table_ref, kk < LEFT + n_seq).T
    pool = _pool(uext, n_seq, POOL_HIST, w_pool_ref, pscale_ref[...])
    mix_s[pl.ds(row0, n_seq), 0:D_ATTN] = attn[0:n_seq, :]
    mix_s[pl.ds(row0, n_seq), D_ATTN:] = pool

    @pl.when(b == pl.num_programs(0) - 1)
    def _():
        y_ref[...] = _finish(x_ref[...], mix_s[...].astype(BF16), p_ref[...], gains_ref, w_out_ref, w_ff1_ref,
                             w_ff2_ref, w_ple_ref, w_gate_ref)


def _sample_layer(x, p, cache_k, cache_v, state_pool, table, gains, pscale, weights):
    batch, n_seq, _ = x.shape
    rows = batch * n_seq
    assert cache_k.shape[1] == LEFT and HIST_ROWS <= n_seq <= SAMPLE_TQ and n_seq % 8 == 0
    full = lambda shape: pl.BlockSpec(shape, lambda b: (0,) * len(shape))
    per_b = lambda shape: pl.BlockSpec((1,) + shape, lambda b: (b, 0, 0))
    return pl.pallas_call(
        functools.partial(_sample_kernel, n_seq),
        grid=(batch,),
        in_specs=[full((rows, D_MODEL)), full((rows, D_PLE)), per_b((LEFT, D_ATTN)), per_b((LEFT, D_ATTN)),
                  per_b((HIST_ROWS, D_POOL)), full((N_HEADS, SAMPLE_LK, SAMPLE_TQ))]
        + [_resident(a) for a in (gains, pscale) + tuple(weights)],
        out_specs=[full((rows, D_MODEL)), full((rows, D_ATTN)), full((rows, D_ATTN)), per_b((HIST_ROWS, D_POOL))],
        out_shape=[jax.ShapeDtypeStruct((rows, D_MODEL), F32), jax.ShapeDtypeStruct((rows, D_ATTN), F32),
                   jax.ShapeDtypeStruct((rows, D_ATTN), F32),
                   jax.ShapeDtypeStruct((batch, HIST_ROWS, D_POOL), F32)],
        scratch_shapes=[pltpu.VMEM((rows, D_ATTN), F32), pltpu.VMEM((rows, D_ATTN + D_POOL), F32),
                        pltpu.VMEM((SAMPLE_LK, D_ATTN), BF16), pltpu.VMEM((D_ATTN, SAMPLE_LK), BF16),
                        pltpu.VMEM((HIST_ROWS + n_seq, D_POOL), F32)],
        compiler_params=pltpu.CompilerParams(dimension_semantics=("arbitrary",),
                                             vmem_limit_bytes=VMEM_LIMIT_BYTES),
        name="sample_layer",
    )(x.reshape(rows, D_MODEL), p.reshape(rows, D_PLE), cache_k, cache_v, state_pool, table, gains, pscale,
      *weights)


def kernel(x_prompt, x_sample, cache_k, cache_v, state_pool, p_prompt, p_sample, g_mix_pre, g_mix_post,
           g_ff_pre, g_ff_post, g_ple_post, w_in, rel_bias, w_pool, pool_scale, w_out, w_ff1, w_ff2, w_ple,
           w_ple_gate):
    depth = w_in.shape[0]
    batch, seq, _ = x_prompt.shape
    dec_batch, dec_seq, _ = x_sample.shape
    xp, xs = x_prompt, x_sample
    outs = [[] for _ in range(6)]
    for i in range(depth):
        table = _bias_table(rel_bias[i])
        gains = jnp.concatenate(
            [g_mix_pre[i:i + 1], g_mix_post[i:i + 1], g_ff_pre[i:i + 1], g_ff_post[i:i + 1], g_ple_post[i:i + 1],
             jnp.zeros((3, D_MODEL), F32)], axis=0)
        pscale = pool_scale[i:i + 1]
        weights = tuple(_pack_rows(w[i].astype(BF16)) for w in (w_in, w_pool, w_out, w_ff1, w_ff2, w_ple, w_ple_gate))
        xp, kp, vp, up = _prompt_layer(xp, p_prompt[i], table, gains, pscale, weights)
        sp = jnp.pad(state_pool[i], ((0, 0), (HIST_ROWS - POOL_HIST, 0), (0, 0)))
        xs2, kn, vn, un = _sample_layer(xs, p_sample[i], cache_k[i].reshape(dec_batch, LEFT, D_ATTN),
                                        cache_v[i].reshape(dec_batch, LEFT, D_ATTN), sp, table, gains, pscale,
                                        weights)
        xs = xs2.reshape(dec_batch, dec_seq, D_MODEL)
        outs[0].append(kp.reshape(batch, LEFT, N_HEADS, HEAD_DIM))
        outs[1].append(vp.reshape(batch, LEFT, N_HEADS, HEAD_DIM))
        outs[2].append(up[:, HIST_ROWS - POOL_HIST:, :])
        outs[3].append(kn.reshape(dec_batch, dec_seq, N_HEADS, HEAD_DIM))
        outs[4].append(vn.reshape(dec_batch, dec_seq, N_HEADS, HEAD_DIM))
        outs[5].append(un[:, HIST_ROWS - POOL_HIST:, :])
    return (xp, xs) + tuple(jnp.stack(o) for o in outs)
```

```python
import functools

import jax
import jax.numpy as jnp
from jax import lax
from jax.experimental import pallas as pl
from jax.experimental.pallas import tpu as pltpu

D_MODEL = 1024
D_ATTN = 512
D_POOL = 512
HEAD_DIM = 64
N_HEADS = 8
CHUNK = 64
LEFT_CHUNKS = 8
LEFT = LEFT_CHUNKS * CHUNK
REL_CLIP = 128
N_REL = 2 * REL_CLIP + 1
POOL_WINDOWS = (2, 4, 8, 16)
POOL_GROUP_DIM = 128
POOL_HIST = 15
HIST_ROWS = 16
D_FF = 4096
FF_CHUNK = 1024
D_PLE = 256
EPS = 1e-6
NEG_INF = -1e30
ATTN_SCALE = HEAD_DIM ** -0.5
LOG2E = 1.4426950408889634
V_ROWS = HEAD_DIM + 16

TQ = 256
LK = LEFT + TQ
SAMPLE_TQ = 128
SAMPLE_LK = LEFT + SAMPLE_TQ
TABLE_PAD = 256
V7X_VMEM_BYTES = 64 * 1024 * 1024
VMEM_LIMIT_BYTES = V7X_VMEM_BYTES * 7 // 8
POOL_ROUND = 2
TILES_PER_STEP = 2

F32 = jnp.float32
BF16 = jnp.bfloat16


def _bias_table_kernel(rbt_ref, out_ref, buf_a, buf_b):
    far = jnp.broadcast_to(rbt_ref[0, 0:1, :], (TABLE_PAD + LEFT - REL_CLIP, TQ))
    buf_a[0:TABLE_PAD + LEFT - REL_CLIP, :] = far
    buf_b[0:TABLE_PAD, :] = far[0:TABLE_PAD]
    buf_a[TABLE_PAD + LEFT - REL_CLIP:TABLE_PAD + LEFT + REL_CLIP, :] = jnp.broadcast_to(
        rbt_ref[0, 0:2 * REL_CLIP, :], (2 * REL_CLIP, TQ))
    buf_a[TABLE_PAD + LEFT + REL_CLIP:TABLE_PAD + LK, :] = jnp.broadcast_to(
        rbt_ref[0, 2 * REL_CLIP:2 * REL_CLIP + 1, :], (LK - LEFT - REL_CLIP, TQ))

    rows = 128
    qidx = lax.broadcasted_iota(jnp.int32, (rows, TQ), 1)
    src, dst = buf_a, buf_b
    for b in range(TQ.bit_length() - 1):
        s = 1 << b
        bit = (qidx & s) != 0
        for r0 in range(TABLE_PAD, TABLE_PAD + LK, rows):
            dst[r0:r0 + rows, :] = jnp.where(bit, src[r0 - s:r0 - s + rows, :], src[r0:r0 + rows, :])
        src, dst = dst, src

    kk = lax.broadcasted_iota(jnp.int32, (LK, TQ), 0)
    qq = lax.broadcasted_iota(jnp.int32, (LK, TQ), 1)
    d = (kk >> 6) - (qq >> 6)
    out_ref[0] = jnp.where((d >= 0) & (d <= LEFT_CHUNKS), src[TABLE_PAD:TABLE_PAD + LK, :] * LOG2E, NEG_INF)


def _bias_table(rel_bias):
    rbt = jnp.pad(rel_bias[:, ::-1], ((0, 0), (0, 384 - N_REL)))[:, :, None]
    return pl.pallas_call(
        _bias_table_kernel,
        grid=(N_HEADS,),
        in_specs=[pl.BlockSpec((1, 384, 1), lambda h: (h, 0, 0))],
        out_specs=pl.BlockSpec((1, LK, TQ), lambda h: (h, 0, 0)),
        out_shape=jax.ShapeDtypeStruct((N_HEADS, LK, TQ), F32),
        scratch_shapes=[pltpu.VMEM((TABLE_PAD + LK, TQ), F32), pltpu.VMEM((TABLE_PAD + LK, TQ), F32)],
        name="bias_table",
    )(rbt)


def _rmsnorm(x, g):
    y = x * lax.rsqrt(jnp.mean(x * x, axis=-1, keepdims=True) + EPS)
    return y * g


def _dot(a, b):
    return jnp.dot(a, b, preferred_element_type=F32)


def _project(x, gains_ref, w_in_ref):
    h = _rmsnorm(x, gains_ref[0:1, :]).astype(BF16)
    qkvu = _dot(h, w_in_ref[...])
    q = qkvu[:, 0:D_ATTN] * (ATTN_SCALE * LOG2E)
    k = qkvu[:, D_ATTN:2 * D_ATTN]
    v = qkvu[:, 2 * D_ATTN:3 * D_ATTN]
    u = qkvu[:, 3 * D_ATTN:]
    return q, k, v, u


def _attention_t(q_t, kext_ref, vt_ref, table_ref, valid):
    nq = q_t.shape[1]
    row = lax.broadcasted_iota(jnp.int32, (2 * HEAD_DIM, nq), 0)
    scores = []
    for h in range(N_HEADS):
        pair = slice(2 * HEAD_DIM * (h // 2), 2 * HEAD_DIM * (h // 2 + 1))
        own = (row < HEAD_DIM) if h % 2 == 0 else (row >= HEAD_DIM)
        scores.append(_dot(kext_ref[:, pair], jnp.where(own, q_t[pair, :], 0.0).astype(BF16)))
    probs, denoms = [], []
    for h in range(N_HEADS):
        s = jnp.where(valid, scores[h] + table_ref[h], NEG_INF)
        p = jnp.exp2(s - jnp.max(s, axis=0, keepdims=True))
        denoms.append(jnp.sum(p, axis=0, keepdims=True))
        probs.append(p.astype(BF16))
    outs = [_dot(vt_ref[HEAD_DIM * h:HEAD_DIM * (h + 1), :], probs[h]) / denoms[h] for h in range(N_HEADS)]
    return jnp.concatenate(outs, axis=0)


def _pool(uext_ref, n_rows, frames_before, w_pool_ref, pool_scale):
    rows = lax.broadcasted_iota(jnp.int32, (n_rows, POOL_GROUP_DIM), 0)
    outs = []
    for g, w in enumerate(POOL_WINDOWS):
        lo = g * POOL_GROUP_DIM
        acc = uext_ref[0:HIST_ROWS + n_rows, lo:lo + POOL_GROUP_DIM]
        cur = acc[HIST_ROWS:, :]
        span = 1
        while span < w:
            acc = acc + pltpu.roll(acc, span, 0)
            span *= 2
        cnt = jnp.minimum(w, frames_before + rows + 1).astype(F32)
        diff = acc[HIST_ROWS:, :] / cnt - cur
        outs.append(_dot(diff.astype(BF16), w_pool_ref[g]))
    return jnp.concatenate(outs, axis=-1) * pool_scale


def _mix_out(x, mix, gains_ref):
    x = x + _rmsnorm(mix, gains_ref[1:2, :])
    return x, _rmsnorm(x, gains_ref[2:3, :]).astype(BF16)


def _ffn_chunk(c, hn, w_ff1_ref, w_ff2_ref):
    chunk = slice(c * FF_CHUNK, (c + 1) * FF_CHUNK)
    act = jnp.square(jnp.maximum(_dot(hn, w_ff1_ref[:, chunk]), 0.0)).astype(BF16)
    return _dot(act, w_ff2_ref[chunk, :])


def _tail(x, ff, p, gains_ref, w_ple_ref, w_gate_ref):
    x = x + _rmsnorm(ff, gains_ref[3:4, :])
    gate = jax.nn.sigmoid(_dot(x.astype(BF16), w_gate_ref[...]))
    ple = gate * _dot(p.astype(BF16), w_ple_ref[...])
    return x + _rmsnorm(ple, gains_ref[4:5, :])


def _finish(x, mixed, p, gains_ref, w_out_ref, w_ff1_ref, w_ff2_ref, w_ple_ref, w_gate_ref):
    x, hn = _mix_out(x, _dot(mixed, w_out_ref[...]), gains_ref)
    ff = None
    for c in range(D_FF // FF_CHUNK):
        part = _ffn_chunk(c, hn, w_ff1_ref, w_ff2_ref)
        ff = part if ff is None else ff + part
    return _tail(x, ff, p, gains_ref, w_ple_ref, w_gate_ref)


def _mask_lane(h):
    return HEAD_DIM if h % 2 == 0 else 0


def _prompt_tile(n_tiles, n_total, s, x_ref, p_ref, table_ref, gains_ref, pscale_ref, w_in_ref, w_pool_ref,
                 w_out_ref, w_ff1_ref, w_ff2_ref, w_ple_ref, w_gate_ref,
                 y_ref, knew_ref, vnew_ref, unew_ref,
                 kext, vt, uext, xbuf, mixbuf, x1_s, ff_s, wq_s, s_scr, p_scr):
    t = lax.rem(jnp.minimum(s, n_total - 1), n_tiles)

    @pl.when(s == 0)
    def _():
        xbuf[...] = jnp.zeros_like(xbuf)
        mixbuf[...] = jnp.zeros_like(mixbuf)
        x1_s[...] = jnp.zeros_like(x1_s)
        ff_s[...] = jnp.zeros_like(ff_s)

    @pl.when(t == 0)
    def _():
        lane = lax.broadcasted_iota(jnp.int32, (LK, 2 * HEAD_DIM), 1)
        for h in range(N_HEADS):
            kext[h] = jnp.where(lane == _mask_lane(h), NEG_INF, 0.0).astype(BF16)
        vt[...] = jnp.zeros_like(vt)
        uext[0:HIST_ROWS, :] = jnp.zeros((HIST_ROWS, D_POOL), F32)

    mix = _dot(mixbuf[...], w_out_ref[...])

    y_ref[0] = _tail(x1_s[...], ff_s[...], p_ref[0], gains_ref, w_ple_ref, w_gate_ref)

    x_prev, hn = _mix_out(xbuf[...], mix, gains_ref)
    x1_s[...] = x_prev

    def ffn_up(c):
        return jnp.square(jnp.maximum(_dot(hn, w_ff1_ref[:, c * FF_CHUNK:(c + 1) * FF_CHUNK]), 0.0)).astype(BF16)

    def ffn_down(c, cols=slice(None)):
        return _dot(acts[c], w_ff2_ref[c * FF_CHUNK:(c + 1) * FF_CHUNK, cols])

    q, k, v, u = _project(x_ref[0], gains_ref, w_in_ref)
    acts = {0: ffn_up(0), 1: ffn_up(1)}
    knew_ref[0] = k
    vnew_ref[0] = v
    unew_ref[0] = u[TQ - HIST_ROWS:, :]
    uext[HIST_ROWS:HIST_ROWS + TQ, :] = u
    q_t = q.T
    v_t = v.T.astype(BF16)
    row = lax.broadcasted_iota(jnp.int32, (2 * HEAD_DIM, TQ), 0)
    lane = lax.broadcasted_iota(jnp.int32, (TQ, 2 * HEAD_DIM), 1)
    ones_rows = jnp.where(lax.broadcasted_iota(jnp.int32, (V_ROWS - HEAD_DIM, TQ), 0) == 0, 1.0, 0.0).astype(BF16)
    for h in range(N_HEADS):
        lanes = slice(2 * HEAD_DIM * (h // 2), 2 * HEAD_DIM * (h // 2 + 1))
        own = (lane < HEAD_DIM) if h % 2 == 0 else (lane >= HEAD_DIM)
        kext[h, 0:LEFT, :] = kext[h, TQ:LK, :]
        kext[h, LEFT:LK, :] = jnp.where(own, k[:, lanes], 0.0).astype(BF16)
        wq_s[h] = jnp.where(row == _mask_lane(h), 1.0, q_t[lanes, :]).astype(BF16)
        vt[h, :, 0:LEFT] = vt[h, :, TQ:LK]
        vt[h, :, LEFT:LK] = jnp.concatenate([v_t[HEAD_DIM * h:HEAD_DIM * (h + 1), :], ones_rows], axis=0)

    def scores(h):
        half = LK // 2
        s_scr[h % 2, 0:half, :] = _dot(kext[h, 0:half, :], wq_s[h])
        s_scr[h % 2, half:LK, :] = _dot(kext[h, half:LK, :], wq_s[h])

    def values(h):
        o = _dot(vt[h], p_scr[h % 2])
        return o[0:HEAD_DIM, :] / o[HEAD_DIM:HEAD_DIM + 1, :]

    lo, hi = slice(0, D_MODEL // 2), slice(D_MODEL // 2, D_MODEL)
    fillers = [lambda: ("down", ffn_down(0)), lambda: ("down", ffn_down(1)), lambda: ("up", 2, ffn_up(2)),
               lambda: ("up", 3, ffn_up(3)), lambda: ("down", ffn_down(2)), lambda: ("lo", ffn_down(3, lo)),
               lambda: ("hi", ffn_down(3, hi))]
    scores(0)
    scores(1)
    ff = None
    halves = {}
    heads = []
    for h in range(N_HEADS):
        sc = s_scr[h % 2] + table_ref[h]
        p_scr[h % 2] = jnp.exp2(sc - jnp.max(sc, axis=0, keepdims=True)).astype(BF16)
        if h < len(fillers):
            kind, *rest = fillers[h]()
            if kind == "up":
                acts[rest[0]] = rest[1]
            elif kind == "down":
                ff = rest[0] if ff is None else ff + rest[0]
            else:
                halves[kind] = rest[0]
        heads.append(values(h))
        if h + 2 < N_HEADS:
            scores(h + 2)
        if h == POOL_ROUND:
            pool = _pool(uext, TQ, t * TQ, w_pool_ref, pscale_ref[...])
            uext[0:HIST_ROWS, :] = uext[TQ:TQ + HIST_ROWS, :]
            mixbuf[:, D_ATTN:] = pool.astype(BF16)
    ff_s[...] = ff + jnp.concatenate([halves["lo"], halves["hi"]], axis=-1)
    mixbuf[:, 0:D_ATTN] = jnp.concatenate(heads, axis=0).T.astype(BF16)
    xbuf[...] = x_ref[0]


def _prompt_kernel(n_tiles, n_total, x_ref, p_ref, *refs):
    n_const = 10
    consts, (y_ref, knew_ref, vnew_ref, unew_ref), scratch = refs[:n_const], refs[n_const:n_const + 4], refs[n_const + 4:]

    def tile_step(j, carry):
        rows = pl.ds(pl.multiple_of(j * TQ, TQ), TQ)
        _prompt_tile(n_tiles, n_total, TILES_PER_STEP * pl.program_id(0) + j, x_ref.at[pl.ds(j, 1)],
                     p_ref.at[pl.ds(j, 1)], *consts, y_ref.at[pl.ds(j, 1)], knew_ref.at[:, rows],
                     vnew_ref.at[:, rows], unew_ref, *scratch)
        return carry

    lax.fori_loop(0, TILES_PER_STEP, tile_step, 0)


def _resident(a):
    return pl.BlockSpec(a.shape, lambda *_: (0,) * a.ndim, pipeline_mode=pl.Buffered(1))


def _prompt_layer(x, p, table, gains, pscale, weights):
    batch, seq, _ = x.shape
    assert seq % TQ == 0 and seq >= LEFT and D_FF // FF_CHUNK == N_HEADS // 2
    assert LEFT == TILES_PER_STEP * TQ and (seq // TQ) % TILES_PER_STEP == 0
    n_tiles = seq // TQ
    n_total = batch * n_tiles
    n_blocks = n_total // TILES_PER_STEP
    mixing = lambda g: (jnp.minimum(g, n_blocks - 1), 0, 0)
    finishing = lambda g: (jnp.maximum(g - 1, 0), 0, 0)
    per_batch = lambda g: (jnp.minimum(g, n_blocks - 1) * TILES_PER_STEP // n_tiles, 0, 0)
    consts = (table, gains, pscale) + tuple(weights)
    y, k_new, v_new, u_new = pl.pallas_call(
        functools.partial(_prompt_kernel, n_tiles, n_total),
        grid=(n_blocks + 1,),
        in_specs=[pl.BlockSpec((TILES_PER_STEP, TQ, D_MODEL), mixing),
                  pl.BlockSpec((TILES_PER_STEP, TQ, D_PLE), finishing)] + [_resident(a) for a in consts],
        out_specs=[pl.BlockSpec((TILES_PER_STEP, TQ, D_MODEL), finishing), pl.BlockSpec((1, LEFT, D_ATTN), per_batch),
                   pl.BlockSpec((1, LEFT, D_ATTN), per_batch), pl.BlockSpec((1, HIST_ROWS, D_POOL), per_batch)],
        out_shape=[jax.ShapeDtypeStruct((n_total, TQ, D_MODEL), F32),
                   jax.ShapeDtypeStruct((batch, LEFT, D_ATTN), F32),
                   jax.ShapeDtypeStruct((batch, LEFT, D_ATTN), F32),
                   jax.ShapeDtypeStruct((batch, HIST_ROWS, D_POOL), F32)],
        scratch_shapes=[pltpu.VMEM((N_HEADS, LK, 2 * HEAD_DIM), BF16), pltpu.VMEM((N_HEADS, V_ROWS, LK), BF16),
                        pltpu.VMEM((HIST_ROWS + TQ, D_POOL), F32), pltpu.VMEM((TQ, D_MODEL), F32),
                        pltpu.VMEM((TQ, D_ATTN + D_POOL), BF16), pltpu.VMEM((TQ, D_MODEL), F32),
                        pltpu.VMEM((TQ, D_MODEL), F32), pltpu.VMEM((N_HEADS, 2 * HEAD_DIM, TQ), BF16),
                        pltpu.VMEM((2, LK, TQ), F32), pltpu.VMEM((2, LK, TQ), BF16)],
        compiler_params=pltpu.CompilerParams(dimension_semantics=("arbitrary",),
                                             vmem_limit_bytes=VMEM_LIMIT_BYTES),
        name="prompt_layer",
    )(x.reshape(n_total, TQ, D_MODEL), p.reshape(n_total, TQ, D_PLE), *consts)
    return y.reshape(batch, seq, D_MODEL), k_new, v_new, u_new


def _sample_kernel(n_seq, x_ref, p_ref, ck_ref, cv_ref, sp_ref, table_ref, gains_ref, pscale_ref, w_in_ref,
                   w_pool_ref, w_out_ref, w_ff1_ref, w_ff2_ref, w_ple_ref, w_gate_ref,
                   y_ref, knew_ref, vnew_ref, unew_ref, q_s, mix_s, kext, vt, uext):
    b = pl.program_id(0)
    row0 = pl.multiple_of(b * n_seq, n_seq)

    @pl.when(b == 0)
    def _():
        q, k, v, u = _project(x_ref[...], gains_ref, w_in_ref)
        q_s[...] = q
        knew_ref[...] = k
        vnew_ref[...] = v
        mix_s[:, D_ATTN:] = u
        kext[LEFT:SAMPLE_LK, :] = jnp.zeros((SAMPLE_TQ, D_ATTN), BF16)
        vt[:, LEFT:SAMPLE_LK] = jnp.zeros((D_ATTN, SAMPLE_TQ), BF16)

    pad = jnp.zeros((SAMPLE_TQ - n_seq, D_ATTN), F32)
    q = jnp.concatenate([q_s[pl.ds(row0, n_seq), :], pad], axis=0)
    v = jnp.concatenate([vnew_ref[pl.ds(row0, n_seq), :], pad], axis=0)
    u = mix_s[pl.ds(row0, n_seq), D_ATTN:]
    kext[0:LEFT, :] = ck_ref[0].astype(BF16)
    kext[LEFT:LEFT + n_seq, :] = knew_ref[pl.ds(row0, n_seq), :].astype(BF16)
    vt[:, 0:LEFT] = cv_ref[0].T.astype(BF16)
    vt[:, LEFT:SAMPLE_LK] = v.T.astype(BF16)
    uext[0:HIST_ROWS, :] = sp_ref[0]
    uext[HIST_ROWS:HIST_ROWS + n_seq, :] = u
    unew_ref[0] = u[n_seq - HIST_ROWS:, :]

    kk = lax.broadcasted_iota(jnp.int32, (SAMPLE_LK, SAMPLE_TQ), 0)
    attn = _attention_t(q.T, kext, vt, table_ref, kk < LEFT + n_seq).T
    pool = _pool(uext, n_seq, POOL_HIST, w_pool_ref, pscale_ref[...])
    mix_s[pl.ds(row0, n_seq), 0:D_ATTN] = attn[0:n_seq, :]
    mix_s[pl.ds(row0, n_seq), D_ATTN:] = pool

    @pl.when(b == pl.num_programs(0) - 1)
    def _():
        y_ref[...] = _finish(x_ref[...], mix_s[...].astype(BF16), p_ref[...], gains_ref, w_out_ref, w_ff1_ref,
                             w_ff2_ref, w_ple_ref, w_gate_ref)


def _sample_layer(x, p, cache_k, cache_v, state_pool, table, gains, pscale, weights):
    batch, n_seq, _ = x.shape
    rows = batch * n_seq
    assert cache_k.shape[1] == LEFT and HIST_ROWS <= n_seq <= SAMPLE_TQ and n_seq % 8 == 0
    full = lambda shape: pl.BlockSpec(shape, lambda b: (0,) * len(shape))
    per_b = lambda shape: pl.BlockSpec((1,) + shape, lambda b: (b, 0, 0))
    return pl.pallas_call(
        functools.partial(_sample_kernel, n_seq),
        grid=(batch,),
        in_specs=[full((rows, D_MODEL)), full((rows, D_PLE)), per_b((LEFT, D_ATTN)), per_b((LEFT, D_ATTN)),
                  per_b((HIST_ROWS, D_POOL)), full((N_HEADS, SAMPLE_LK, SAMPLE_TQ))]
        + [_resident(a) for a in (gains, pscale) + tuple(weights)],
        out_specs=[full((rows, D_MODEL)), full((rows, D_ATTN)), full((rows, D_ATTN)), per_b((HIST_ROWS, D_POOL))],
        out_shape=[jax.ShapeDtypeStruct((rows, D_MODEL), F32), jax.ShapeDtypeStruct((rows, D_ATTN), F32),
                   jax.ShapeDtypeStruct((rows, D_ATTN), F32),
                   jax.ShapeDtypeStruct((batch, HIST_ROWS, D_POOL), F32)],
        scratch_shapes=[pltpu.VMEM((rows, D_ATTN), F32), pltpu.VMEM((rows, D_ATTN + D_POOL), F32),
                        pltpu.VMEM((SAMPLE_LK, D_ATTN), BF16), pltpu.VMEM((D_ATTN, SAMPLE_LK), BF16),
                        pltpu.VMEM((HIST_ROWS + n_seq, D_POOL), F32)],
        compiler_params=pltpu.CompilerParams(dimension_semantics=("arbitrary",),
                                             vmem_limit_bytes=VMEM_LIMIT_BYTES),
        name="sample_layer",
    )(x.reshape(rows, D_MODEL), p.reshape(rows, D_PLE), cache_k, cache_v, state_pool, table, gains, pscale,
      *weights)


def kernel(x_prompt, x_sample, cache_k, cache_v, state_pool, p_prompt, p_sample, g_mix_pre, g_mix_post,
           g_ff_pre, g_ff_post, g_ple_post, w_in, rel_bias, w_pool, pool_scale, w_out, w_ff1, w_ff2, w_ple,
           w_ple_gate):
    depth = w_in.shape[0]
    batch, seq, _ = x_prompt.shape
    dec_batch, dec_seq, _ = x_sample.shape
    xp, xs = x_prompt, x_sample
    outs = [[] for _ in range(6)]
    for i in range(depth):
        table = _bias_table(rel_bias[i])
        gains = jnp.concatenate(
            [g_mix_pre[i:i + 1], g_mix_post[i:i + 1], g_ff_pre[i:i + 1], g_ff_post[i:i + 1], g_ple_post[i:i + 1],
             jnp.zeros((3, D_MODEL), F32)], axis=0)
        pscale = pool_scale[i:i + 1]
        weights = tuple(w[i].astype(BF16) for w in (w_in, w_pool, w_out, w_ff1, w_ff2, w_ple, w_ple_gate))
        xp, kp, vp, up = _prompt_layer(xp, p_prompt[i], table, gains, pscale, weights)
        sp = jnp.pad(state_pool[i], ((0, 0), (HIST_ROWS - POOL_HIST, 0), (0, 0)))
        xs2, kn, vn, un = _sample_layer(xs, p_sample[i], cache_k[i].reshape(dec_batch, LEFT, D_ATTN),
                                        cache_v[i].reshape(dec_batch, LEFT, D_ATTN), sp, table, gains, pscale,
                                        weights)
        xs = xs2.reshape(dec_batch, dec_seq, D_MODEL)
        outs[0].append(kp.reshape(batch, LEFT, N_HEADS, HEAD_DIM))
        outs[1].append(vp.reshape(batch, LEFT, N_HEADS, HEAD_DIM))
        outs[2].append(up[:, HIST_ROWS - POOL_HIST:, :])
        outs[3].append(kn.reshape(dec_batch, dec_seq, N_HEADS, HEAD_DIM))
        outs[4].append(vn.reshape(dec_batch, dec_seq, N_HEADS, HEAD_DIM))
        outs[5].append(un[:, HIST_ROWS - POOL_HIST:, :])
    return (xp, xs) + tuple(jnp.stack(o) for o in outs)
```

```python
import functools

import jax
import jax.numpy as jnp
from jax import lax
from jax.experimental import pallas as pl
from jax.experimental.pallas import tpu as pltpu

D_MODEL = 1024
D_ATTN = 512
D_POOL = 512
HEAD_DIM = 64
N_HEADS = 8
CHUNK = 64
LEFT_CHUNKS = 8
LEFT = LEFT_CHUNKS * CHUNK
REL_CLIP = 128
N_REL = 2 * REL_CLIP + 1
POOL_WINDOWS = (2, 4, 8, 16)
POOL_GROUP_DIM = 128
POOL_HIST = 15
HIST_ROWS = 16
D_FF = 4096
FF_CHUNK = 512
D_PLE = 256
EPS = 1e-6
NEG_INF = -1e30
ATTN_SCALE = HEAD_DIM ** -0.5
LOG2E = 1.4426950408889634
V_ROWS = HEAD_DIM + 16

TQ = 256
LK = LEFT + TQ
SAMPLE_TQ = 128
SAMPLE_LK = LEFT + SAMPLE_TQ
TABLE_PAD = 256
V7X_VMEM_BYTES = 64 * 1024 * 1024
VMEM_LIMIT_BYTES = V7X_VMEM_BYTES * 7 // 8
POOL_ROUND = 2
TILES_PER_STEP = 2

F32 = jnp.float32
BF16 = jnp.bfloat16


def _bias_table_kernel(rbt_ref, out_ref, buf_a, buf_b):
    far = jnp.broadcast_to(rbt_ref[0, 0:1, :], (TABLE_PAD + LEFT - REL_CLIP, TQ))
    buf_a[0:TABLE_PAD + LEFT - REL_CLIP, :] = far
    buf_b[0:TABLE_PAD, :] = far[0:TABLE_PAD]
    buf_a[TABLE_PAD + LEFT - REL_CLIP:TABLE_PAD + LEFT + REL_CLIP, :] = jnp.broadcast_to(
        rbt_ref[0, 0:2 * REL_CLIP, :], (2 * REL_CLIP, TQ))
    buf_a[TABLE_PAD + LEFT + REL_CLIP:TABLE_PAD + LK, :] = jnp.broadcast_to(
        rbt_ref[0, 2 * REL_CLIP:2 * REL_CLIP + 1, :], (LK - LEFT - REL_CLIP, TQ))

    rows = 128
    qidx = lax.broadcasted_iota(jnp.int32, (rows, TQ), 1)
    src, dst = buf_a, buf_b
    for b in range(TQ.bit_length() - 1):
        s = 1 << b
        bit = (qidx & s) != 0
        for r0 in range(TABLE_PAD, TABLE_PAD + LK, rows):
            dst[r0:r0 + rows, :] = jnp.where(bit, src[r0 - s:r0 - s + rows, :], src[r0:r0 + rows, :])
        src, dst = dst, src

    kk = lax.broadcasted_iota(jnp.int32, (LK, TQ), 0)
    qq = lax.broadcasted_iota(jnp.int32, (LK, TQ), 1)
    d = (kk >> 6) - (qq >> 6)
    out_ref[0] = jnp.where((d >= 0) & (d <= LEFT_CHUNKS), src[TABLE_PAD:TABLE_PAD + LK, :] * LOG2E, NEG_INF)


def _bias_table(rel_bias):
    rbt = jnp.pad(rel_bias[:, ::-1], ((0, 0), (0, 384 - N_REL)))[:, :, None]
    return pl.pallas_call(
        _bias_table_kernel,
        grid=(N_HEADS,),
        in_specs=[pl.BlockSpec((1, 384, 1), lambda h: (h, 0, 0))],
        out_specs=pl.BlockSpec((1, LK, TQ), lambda h: (h, 0, 0)),
        out_shape=jax.ShapeDtypeStruct((N_HEADS, LK, TQ), F32),
        scratch_shapes=[pltpu.VMEM((TABLE_PAD + LK, TQ), F32), pltpu.VMEM((TABLE_PAD + LK, TQ), F32)],
        name="bias_table",
    )(rbt)


def _rmsnorm(x, g):
    y = x * lax.rsqrt(jnp.mean(x * x, axis=-1, keepdims=True) + EPS)
    return y * g


def _dot(a, b):
    return jnp.dot(a, b, preferred_element_type=F32)


def _project(x, gains_ref, w_in_ref):
    h = _rmsnorm(x, gains_ref[0:1, :]).astype(BF16)
    qkvu = _dot(h, w_in_ref[...])
    q = qkvu[:, 0:D_ATTN] * (ATTN_SCALE * LOG2E)
    k = qkvu[:, D_ATTN:2 * D_ATTN]
    v = qkvu[:, 2 * D_ATTN:3 * D_ATTN]
    u = qkvu[:, 3 * D_ATTN:]
    return q, k, v, u


def _attention_t(q_t, kext_ref, vt_ref, table_ref, valid):
    nq = q_t.shape[1]
    row = lax.broadcasted_iota(jnp.int32, (2 * HEAD_DIM, nq), 0)
    scores = []
    for h in range(N_HEADS):
        pair = slice(2 * HEAD_DIM * (h // 2), 2 * HEAD_DIM * (h // 2 + 1))
        own = (row < HEAD_DIM) if h % 2 == 0 else (row >= HEAD_DIM)
        scores.append(_dot(kext_ref[:, pair], jnp.where(own, q_t[pair, :], 0.0).astype(BF16)))
    probs, denoms = [], []
    for h in range(N_HEADS):
        s = jnp.where(valid, scores[h] + table_ref[h], NEG_INF)
        p = jnp.exp2(s - jnp.max(s, axis=0, keepdims=True))
        denoms.append(jnp.sum(p, axis=0, keepdims=True))
        probs.append(p.astype(BF16))
    outs = [_dot(vt_ref[HEAD_DIM * h:HEAD_DIM * (h + 1), :], probs[h]) / denoms[h] for h in range(N_HEADS)]
    return jnp.concatenate(outs, axis=0)


def _pool(uext_ref, n_rows, frames_before, w_pool_ref, pool_scale):
    rows = lax.broadcasted_iota(jnp.int32, (n_rows, POOL_GROUP_DIM), 0)
    outs = []
    for g, w in enumerate(POOL_WINDOWS):
        lo = g * POOL_GROUP_DIM
        acc = uext_ref[0:HIST_ROWS + n_rows, lo:lo + POOL_GROUP_DIM]
        cur = acc[HIST_ROWS:, :]
        span = 1
        while span < w:
            acc = acc + pltpu.roll(acc, span, 0)
            span *= 2
        cnt = jnp.minimum(w, frames_before + rows + 1).astype(F32)
        diff = acc[HIST_ROWS:, :] / cnt - cur
        outs.append(_dot(diff.astype(BF16), w_pool_ref[g]))
    return jnp.concatenate(outs, axis=-1) * pool_scale


def _mix_out(x, mix, gains_ref):
    x = x + _rmsnorm(mix, gains_ref[1:2, :])
    return x, _rmsnorm(x, gains_ref[2:3, :]).astype(BF16)


def _ffn_chunk(c, hn, w_ff1_ref, w_ff2_ref):
    chunk = slice(c * FF_CHUNK, (c + 1) * FF_CHUNK)
    act = jnp.square(jnp.maximum(_dot(hn, w_ff1_ref[:, chunk]), 0.0)).astype(BF16)
    return _dot(act, w_ff2_ref[chunk, :])


def _tail(x, ff, p, gains_ref, w_ple_ref, w_gate_ref):
    x = x + _rmsnorm(ff, gains_ref[3:4, :])
    gate = jax.nn.sigmoid(_dot(x.astype(BF16), w_gate_ref[...]))
    ple = gate * _dot(p.astype(BF16), w_ple_ref[...])
    return x + _rmsnorm(ple, gains_ref[4:5, :])


def _finish(x, mixed, p, gains_ref, w_out_ref, w_ff1_ref, w_ff2_ref, w_ple_ref, w_gate_ref):
    x, hn = _mix_out(x, _dot(mixed, w_out_ref[...]), gains_ref)
    ff = None
    for c in range(D_FF // FF_CHUNK):
        part = _ffn_chunk(c, hn, w_ff1_ref, w_ff2_ref)
        ff = part if ff is None else ff + part
    return _tail(x, ff, p, gains_ref, w_ple_ref, w_gate_ref)


def _mask_lane(h):
    return HEAD_DIM if h % 2 == 0 else 0


def _prompt_tile(n_tiles, n_total, s, x_ref, p_ref, table_ref, gains_ref, pscale_ref, w_in_ref, w_pool_ref,
                 w_out_ref, w_ff1_ref, w_ff2_ref, w_ple_ref, w_gate_ref,
                 y_ref, knew_ref, vnew_ref, unew_ref,
                 kext, vt, uext, xbuf, mixbuf, x1_s, ff_s, wq_s, s_scr, p_scr):
    t = lax.rem(jnp.minimum(s, n_total - 1), n_tiles)

    @pl.when(s == 0)
    def _():
        xbuf[...] = jnp.zeros_like(xbuf)
        mixbuf[...] = jnp.zeros_like(mixbuf)
        x1_s[...] = jnp.zeros_like(x1_s)
        ff_s[...] = jnp.zeros_like(ff_s)

    @pl.when(t == 0)
    def _():
        lane = lax.broadcasted_iota(jnp.int32, (LK, 2 * HEAD_DIM), 1)
        for h in range(N_HEADS):
            kext[h] = jnp.where(lane == _mask_lane(h), NEG_INF, 0.0).astype(BF16)
        vt[...] = jnp.zeros_like(vt)
        uext[0:HIST_ROWS, :] = jnp.zeros((HIST_ROWS, D_POOL), F32)

    mix = _dot(mixbuf[...], w_out_ref[...])

    y_ref[0] = _tail(x1_s[...], ff_s[...], p_ref[0], gains_ref, w_ple_ref, w_gate_ref)

    x_prev, hn = _mix_out(xbuf[...], mix, gains_ref)
    x1_s[...] = x_prev

    def ffn_up(c):
        return jnp.square(jnp.maximum(_dot(hn, w_ff1_ref[:, c * FF_CHUNK:(c + 1) * FF_CHUNK]), 0.0)).astype(BF16)

    def ffn_down(c):
        return _dot(acts[c], w_ff2_ref[c * FF_CHUNK:(c + 1) * FF_CHUNK, :])

    def ffn_pieces(*ops):
        def run():
            total = None
            for kind, c in ops:
                if kind == "u":
                    acts[c] = ffn_up(c)
                else:
                    part = ffn_down(c)
                    total = part if total is None else total + part
            return total
        return run

    q, k, v, u = _project(x_ref[0], gains_ref, w_in_ref)
    acts = {0: ffn_up(0), 1: ffn_up(1)}
    knew_ref[0] = k
    vnew_ref[0] = v
    unew_ref[0] = u[TQ - HIST_ROWS:, :]
    uext[HIST_ROWS:HIST_ROWS + TQ, :] = u
    q_t = q.T
    v_t = v.T.astype(BF16)
    row = lax.broadcasted_iota(jnp.int32, (2 * HEAD_DIM, TQ), 0)
    lane = lax.broadcasted_iota(jnp.int32, (TQ, 2 * HEAD_DIM), 1)
    ones_rows = jnp.where(lax.broadcasted_iota(jnp.int32, (V_ROWS - HEAD_DIM, TQ), 0) == 0, 1.0, 0.0).astype(BF16)
    for h in range(N_HEADS):
        lanes = slice(2 * HEAD_DIM * (h // 2), 2 * HEAD_DIM * (h // 2 + 1))
        own = (lane < HEAD_DIM) if h % 2 == 0 else (lane >= HEAD_DIM)
        kext[h, 0:LEFT, :] = kext[h, TQ:LK, :]
        kext[h, LEFT:LK, :] = jnp.where(own, k[:, lanes], 0.0).astype(BF16)
        wq_s[h] = jnp.where(row == _mask_lane(h), 1.0, q_t[lanes, :]).astype(BF16)
        vt[h, :, 0:LEFT] = vt[h, :, TQ:LK]
        vt[h, :, LEFT:LK] = jnp.concatenate([v_t[HEAD_DIM * h:HEAD_DIM * (h + 1), :], ones_rows], axis=0)

    def scores(h):
        half = LK // 2
        s_scr[h % 2, 0:half, :] = _dot(kext[h, 0:half, :], wq_s[h])
        s_scr[h % 2, half:LK, :] = _dot(kext[h, half:LK, :], wq_s[h])

    def values(h):
        o = _dot(vt[h], p_scr[h % 2])
        return o[0:HEAD_DIM, :] / o[HEAD_DIM:HEAD_DIM + 1, :]

    fillers = [ffn_pieces(("d", 0), ("u", 2)), ffn_pieces(("d", 1), ("u", 3)), ffn_pieces(("d", 2), ("u", 4)),
               ffn_pieces(("d", 3), ("u", 5)), ffn_pieces(("d", 4), ("u", 6)), ffn_pieces(("d", 5), ("u", 7)),
               ffn_pieces(("d", 6), ("d", 7))]
    scores(0)
    scores(1)
    ff = None
    heads = []
    for h in range(N_HEADS):
        sc = s_scr[h % 2] + table_ref[h]
        p_scr[h % 2] = jnp.exp2(sc - jnp.max(sc, axis=0, keepdims=True)).astype(BF16)
        if h < len(fillers):
            part = fillers[h]()
            ff = part if ff is None else ff + part
        heads.append(values(h))
        if h + 2 < N_HEADS:
            scores(h + 2)
        if h == POOL_ROUND:
            pool = _pool(uext, TQ, t * TQ, w_pool_ref, pscale_ref[...])
            uext[0:HIST_ROWS, :] = uext[TQ:TQ + HIST_ROWS, :]
            mixbuf[:, D_ATTN:] = pool.astype(BF16)
    ff_s[...] = ff
    mixbuf[:, 0:D_ATTN] = jnp.concatenate(heads, axis=0).T.astype(BF16)
    xbuf[...] = x_ref[0]


def _prompt_kernel(n_tiles, n_total, x_ref, p_ref, *refs):
    n_const = 10
    consts, (y_ref, knew_ref, vnew_ref, unew_ref), scratch = refs[:n_const], refs[n_const:n_const + 4], refs[n_const + 4:]

    def tile_step(j, carry):
        rows = pl.ds(pl.multiple_of(j * TQ, TQ), TQ)
        _prompt_tile(n_tiles, n_total, TILES_PER_STEP * pl.program_id(0) + j, x_ref.at[pl.ds(j, 1)],
                     p_ref.at[pl.ds(j, 1)], *consts, y_ref.at[pl.ds(j, 1)], knew_ref.at[:, rows],
                     vnew_ref.at[:, rows], unew_ref, *scratch)
        return carry

    lax.fori_loop(0, TILES_PER_STEP, tile_step, 0)


def _resident(a):
    return pl.BlockSpec(a.shape, lambda *_: (0,) * a.ndim, pipeline_mode=pl.Buffered(1))


def _prompt_layer(x, p, table, gains, pscale, weights):
    batch, seq, _ = x.shape
    assert seq % TQ == 0 and seq >= LEFT and D_FF // FF_CHUNK == N_HEADS
    assert LEFT == TILES_PER_STEP * TQ and (seq // TQ) % TILES_PER_STEP == 0
    n_tiles = seq // TQ
    n_total = batch * n_tiles
    n_blocks = n_total // TILES_PER_STEP
    mixing = lambda g: (jnp.minimum(g, n_blocks - 1), 0, 0)
    finishing = lambda g: (jnp.maximum(g - 1, 0), 0, 0)
    per_batch = lambda g: (jnp.minimum(g, n_blocks - 1) * TILES_PER_STEP // n_tiles, 0, 0)
    consts = (table, gains, pscale) + tuple(weights)
    y, k_new, v_new, u_new = pl.pallas_call(
        functools.partial(_prompt_kernel, n_tiles, n_total),
        grid=(n_blocks + 1,),
        in_specs=[pl.BlockSpec((TILES_PER_STEP, TQ, D_MODEL), mixing),
                  pl.BlockSpec((TILES_PER_STEP, TQ, D_PLE), finishing)] + [_resident(a) for a in consts],
        out_specs=[pl.BlockSpec((TILES_PER_STEP, TQ, D_MODEL), finishing), pl.BlockSpec((1, LEFT, D_ATTN), per_batch),
                   pl.BlockSpec((1, LEFT, D_ATTN), per_batch), pl.BlockSpec((1, HIST_ROWS, D_POOL), per_batch)],
        out_shape=[jax.ShapeDtypeStruct((n_total, TQ, D_MODEL), F32),
                   jax.ShapeDtypeStruct((batch, LEFT, D_ATTN), F32),
                   jax.ShapeDtypeStruct((batch, LEFT, D_ATTN), F32),
                   jax.ShapeDtypeStruct((batch, HIST_ROWS, D_POOL), F32)],
        scratch_shapes=[pltpu.VMEM((N_HEADS, LK, 2 * HEAD_DIM), BF16), pltpu.VMEM((N_HEADS, V_ROWS, LK), BF16),
                        pltpu.VMEM((HIST_ROWS + TQ, D_POOL), F32), pltpu.VMEM((TQ, D_MODEL), F32),
                        pltpu.VMEM((TQ, D_ATTN + D_POOL), BF16), pltpu.VMEM((TQ, D_MODEL), F32),
                        pltpu.VMEM((TQ, D_MODEL), F32), pltpu.VMEM((N_HEADS, 2 * HEAD_DIM, TQ), BF16),
                        pltpu.VMEM((2, LK, TQ), F32), pltpu.VMEM((2, LK, TQ), BF16)],
        compiler_params=pltpu.CompilerParams(dimension_semantics=("arbitrary",),
                                             vmem_limit_bytes=VMEM_LIMIT_BYTES),
        name="prompt_layer",
    )(x.reshape(n_total, TQ, D_MODEL), p.reshape(n_total, TQ, D_PLE), *consts)
    return y.reshape(batch, seq, D_MODEL), k_new, v_new, u_new


def _sample_kernel(n_seq, x_ref, p_ref, ck_ref, cv_ref, sp_ref, table_ref, gains_ref, pscale_ref, w_in_ref,
                   w_pool_ref, w_out_ref, w_ff1_ref, w_ff2_ref, w_ple_ref, w_gate_ref,
                   y_ref, knew_ref, vnew_ref, unew_ref, q_s, mix_s, kext, vt, uext):
    b = pl.program_id(0)
    row0 = pl.multiple_of(b * n_seq, n_seq)

    @pl.when(b == 0)
    def _():
        q, k, v, u = _project(x_ref[...], gains_ref, w_in_ref)
        q_s[...] = q
        knew_ref[...] = k
        vnew_ref[...] = v
        mix_s[:, D_ATTN:] = u
        kext[LEFT:SAMPLE_LK, :] = jnp.zeros((SAMPLE_TQ, D_ATTN), BF16)
        vt[:, LEFT:SAMPLE_LK] = jnp.zeros((D_ATTN, SAMPLE_TQ), BF16)

    pad = jnp.zeros((SAMPLE_TQ - n_seq, D_ATTN), F32)
    q = jnp.concatenate([q_s[pl.ds(row0, n_seq), :], pad], axis=0)
    v = jnp.concatenate([vnew_ref[pl.ds(row0, n_seq), :], pad], axis=0)
    u = mix_s[pl.ds(row0, n_seq), D_ATTN:]
    kext[0:LEFT, :] = ck_ref[0].astype(BF16)
    kext[LEFT:LEFT + n_seq, :] = knew_ref[pl.ds(row0, n_seq), :].astype(BF16)
    vt[:, 0:LEFT] = cv_ref[0].T.astype(BF16)
    vt[:, LEFT:SAMPLE_LK] = v.T.astype(BF16)
    uext[0:HIST_ROWS, :] = sp_ref[0]
    uext[HIST_ROWS:HIST_ROWS + n_seq, :] = u
    unew_ref[0] = u[n_seq - HIST_ROWS:, :]

    kk = lax.broadcasted_iota(jnp.int32, (SAMPLE_LK, SAMPLE_TQ), 0)
    attn = _attention_t(q.T, kext, vt, table_ref, kk < LEFT + n_seq).T
    pool = _pool(uext, n_seq, POOL_HIST, w_pool_ref, pscale_ref[...])
    mix_s[pl.ds(row0, n_seq), 0:D_ATTN] = attn[0:n_seq, :]
    mix_s[pl.ds(row0, n_seq), D_ATTN:] = pool

    @pl.when(b == pl.num_programs(0) - 1)
    def _():
        y_ref[...] = _finish(x_ref[...], mix_s[...].astype(BF16), p_ref[...], gains_ref, w_out_ref, w_ff1_ref,
                             w_ff2_ref, w_ple_ref, w_gate_ref)


def _sample_layer(x, p, cache_k, cache_v, state_pool, table, gains, pscale, weights):
    batch, n_seq, _ = x.shape
    rows = batch * n_seq
    assert cache_k.shape[1] == LEFT and HIST_ROWS <= n_seq <= SAMPLE_TQ and n_seq % 8 == 0
    full = lambda shape: pl.BlockSpec(shape, lambda b: (0,) * len(shape))
    per_b = lambda shape: pl.BlockSpec((1,) + shape, lambda b: (b, 0, 0))
    return pl.pallas_call(
        functools.partial(_sample_kernel, n_seq),
        grid=(batch,),
        in_specs=[full((rows, D_MODEL)), full((rows, D_PLE)), per_b((LEFT, D_ATTN)), per_b((LEFT, D_ATTN)),
                  per_b((HIST_ROWS, D_POOL)), full((N_HEADS, SAMPLE_LK, SAMPLE_TQ))]
        + [_resident(a) for a in (gains, pscale) + tuple(weights)],
        out_specs=[full((rows, D_MODEL)), full((rows, D_ATTN)), full((rows, D_ATTN)), per_b((HIST_ROWS, D_POOL))],
        out_shape=[jax.ShapeDtypeStruct((rows, D_MODEL), F32), jax.ShapeDtypeStruct((rows, D_ATTN), F32),
                   jax.ShapeDtypeStruct((rows, D_ATTN), F32),
                   jax.ShapeDtypeStruct((batch, HIST_ROWS, D_POOL), F32)],
        scratch_shapes=[pltpu.VMEM((rows, D_ATTN), F32), pltpu.VMEM((rows, D_ATTN + D_POOL), F32),
                        pltpu.VMEM((SAMPLE_LK, D_ATTN), BF16), pltpu.VMEM((D_ATTN, SAMPLE_LK), BF16),
                        pltpu.VMEM((HIST_ROWS + n_seq, D_POOL), F32)],
        compiler_params=pltpu.CompilerParams(dimension_semantics=("arbitrary",),
                                             vmem_limit_bytes=VMEM_LIMIT_BYTES),
        name="sample_layer",
    )(x.reshape(rows, D_MODEL), p.reshape(rows, D_PLE), cache_k, cache_v, state_pool, table, gains, pscale,
      *weights)


def kernel(x_prompt, x_sample, cache_k, cache_v, state_pool, p_prompt, p_sample, g_mix_pre, g_mix_post,
           g_ff_pre, g_ff_post, g_ple_post, w_in, rel_bias, w_pool, pool_scale, w_out, w_ff1, w_ff2, w_ple,
           w_ple_gate):
    depth = w_in.shape[0]
    batch, seq, _ = x_prompt.shape
    dec_batch, dec_seq, _ = x_sample.shape
    xp, xs = x_prompt, x_sample
    outs = [[] for _ in range(6)]
    for i in range(depth):
        table = _bias_table(rel_bias[i])
        gains = jnp.concatenate(
            [g_mix_pre[i:i + 1], g_mix_post[i:i + 1], g_ff_pre[i:i + 1], g_ff_post[i:i + 1], g_ple_post[i:i + 1],
             jnp.zeros((3, D_MODEL), F32)], axis=0)
        pscale = pool_scale[i:i + 1]
        weights = tuple(w[i].astype(BF16) for w in (w_in, w_pool, w_out, w_ff1, w_ff2, w_ple, w_ple_gate))
        xp, kp, vp, up = _prompt_layer(xp, p_prompt[i], table, gains, pscale, weights)
        sp = jnp.pad(state_pool[i], ((0, 0), (HIST_ROWS - POOL_HIST, 0), (0, 0)))
        xs2, kn, vn, un = _sample_layer(xs, p_sample[i], cache_k[i].reshape(dec_batch, LEFT, D_ATTN),
                                        cache_v[i].reshape(dec_batch, LEFT, D_ATTN), sp, table, gains, pscale,
                                        weights)
        xs = xs2.reshape(dec_batch, dec_seq, D_MODEL)
        outs[0].append(kp.reshape(batch, LEFT, N_HEADS, HEAD_DIM))
        outs[1].append(vp.reshape(batch, LEFT, N_HEADS, HEAD_DIM))
        outs[2].append(up[:, HIST_ROWS - POOL_HIST:, :])
        outs[3].append(kn.reshape(dec_batch, dec_seq, N_HEADS, HEAD_DIM))
        outs[4].append(vn.reshape(dec_batch, dec_seq, N_HEADS, HEAD_DIM))
        outs[5].append(un[:, HIST_ROWS - POOL_HIST:, :])
    return (xp, xs) + tuple(jnp.stack(o) for o in outs)
```

```python
import functools

import jax
import jax.numpy as jnp
from jax import lax
from jax.experimental import pallas as pl
from jax.experimental.pallas import tpu as pltpu

D_MODEL = 1024
D_ATTN = 512
D_POOL = 512
HEAD_DIM = 64
N_HEADS = 8
CHUNK = 64
LEFT_CHUNKS = 8
LEFT = LEFT_CHUNKS * CHUNK
REL_CLIP = 128
N_REL = 2 * REL_CLIP + 1
POOL_WINDOWS = (2, 4, 8, 16)
POOL_GROUP_DIM = 128
POOL_HIST = 15
HIST_ROWS = 16
D_FF = 4096
FF_CHUNK = 512
D_PLE = 256
EPS = 1e-6
NEG_INF = -1e30
ATTN_SCALE = HEAD_DIM ** -0.5
LOG2E = 1.4426950408889634
V_ROWS = HEAD_DIM + 16

TQ = 256
LK = LEFT + TQ
SAMPLE_TQ = 128
SAMPLE_LK = LEFT + SAMPLE_TQ
TABLE_PAD = 256
V7X_VMEM_BYTES = 64 * 1024 * 1024
VMEM_LIMIT_BYTES = V7X_VMEM_BYTES * 7 // 8
POOL_ROUND = 2

F32 = jnp.float32
BF16 = jnp.bfloat16


def _bias_table_kernel(rbt_ref, out_ref, buf_a, buf_b):
    far = jnp.broadcast_to(rbt_ref[0, 0:1, :], (TABLE_PAD + LEFT - REL_CLIP, TQ))
    buf_a[0:TABLE_PAD + LEFT - REL_CLIP, :] = far
    buf_b[0:TABLE_PAD, :] = far[0:TABLE_PAD]
    buf_a[TABLE_PAD + LEFT - REL_CLIP:TABLE_PAD + LEFT + REL_CLIP, :] = jnp.broadcast_to(
        rbt_ref[0, 0:2 * REL_CLIP, :], (2 * REL_CLIP, TQ))
    buf_a[TABLE_PAD + LEFT + REL_CLIP:TABLE_PAD + LK, :] = jnp.broadcast_to(
        rbt_ref[0, 2 * REL_CLIP:2 * REL_CLIP + 1, :], (LK - LEFT - REL_CLIP, TQ))

    rows = 128
    qidx = lax.broadcasted_iota(jnp.int32, (rows, TQ), 1)
    src, dst = buf_a, buf_b
    for b in range(TQ.bit_length() - 1):
        s = 1 << b
        bit = (qidx & s) != 0
        for r0 in range(TABLE_PAD, TABLE_PAD + LK, rows):
            dst[r0:r0 + rows, :] = jnp.where(bit, src[r0 - s:r0 - s + rows, :], src[r0:r0 + rows, :])
        src, dst = dst, src

    kk = lax.broadcasted_iota(jnp.int32, (LK, TQ), 0)
    qq = lax.broadcasted_iota(jnp.int32, (LK, TQ), 1)
    d = (kk >> 6) - (qq >> 6)
    out_ref[0] = jnp.where((d >= 0) & (d <= LEFT_CHUNKS), src[TABLE_PAD:TABLE_PAD + LK, :] * LOG2E, NEG_INF)


def _bias_table(rel_bias):
    rbt = jnp.pad(rel_bias[:, ::-1], ((0, 0), (0, 384 - N_REL)))[:, :, None]
    return pl.pallas_call(
        _bias_table_kernel,
        grid=(N_HEADS,),
        in_specs=[pl.BlockSpec((1, 384, 1), lambda h: (h, 0, 0))],
        out_specs=pl.BlockSpec((1, LK, TQ), lambda h: (h, 0, 0)),
        out_shape=jax.ShapeDtypeStruct((N_HEADS, LK, TQ), F32),
        scratch_shapes=[pltpu.VMEM((TABLE_PAD + LK, TQ), F32), pltpu.VMEM((TABLE_PAD + LK, TQ), F32)],
        name="bias_table",
    )(rbt)


def _rmsnorm(x, g):
    y = x * lax.rsqrt(jnp.mean(x * x, axis=-1, keepdims=True) + EPS)
    return y * g


def _dot(a, b):
    return jnp.dot(a, b, preferred_element_type=F32)


def _project(x, gains_ref, w_in_ref):
    h = _rmsnorm(x, gains_ref[0:1, :]).astype(BF16)
    qkvu = _dot(h, w_in_ref[...])
    q = qkvu[:, 0:D_ATTN] * (ATTN_SCALE * LOG2E)
    k = qkvu[:, D_ATTN:2 * D_ATTN]
    v = qkvu[:, 2 * D_ATTN:3 * D_ATTN]
    u = qkvu[:, 3 * D_ATTN:]
    return q, k, v, u


def _attention_t(q_t, kext_ref, vt_ref, table_ref, valid):
    nq = q_t.shape[1]
    row = lax.broadcasted_iota(jnp.int32, (2 * HEAD_DIM, nq), 0)
    scores = []
    for h in range(N_HEADS):
        pair = slice(2 * HEAD_DIM * (h // 2), 2 * HEAD_DIM * (h // 2 + 1))
        own = (row < HEAD_DIM) if h % 2 == 0 else (row >= HEAD_DIM)
        scores.append(_dot(kext_ref[:, pair], jnp.where(own, q_t[pair, :], 0.0).astype(BF16)))
    probs, denoms = [], []
    for h in range(N_HEADS):
        s = jnp.where(valid, scores[h] + table_ref[h], NEG_INF)
        p = jnp.exp2(s - jnp.max(s, axis=0, keepdims=True))
        denoms.append(jnp.sum(p, axis=0, keepdims=True))
        probs.append(p.astype(BF16))
    outs = [_dot(vt_ref[HEAD_DIM * h:HEAD_DIM * (h + 1), :], probs[h]) / denoms[h] for h in range(N_HEADS)]
    return jnp.concatenate(outs, axis=0)


def _pool(uext_ref, n_rows, frames_before, w_pool_ref, pool_scale):
    rows = lax.broadcasted_iota(jnp.int32, (n_rows, POOL_GROUP_DIM), 0)
    outs = []
    for g, w in enumerate(POOL_WINDOWS):
        lo = g * POOL_GROUP_DIM
        acc = uext_ref[0:HIST_ROWS + n_rows, lo:lo + POOL_GROUP_DIM]
        cur = acc[HIST_ROWS:, :]
        span = 1
        while span < w:
            acc = acc + pltpu.roll(acc, span, 0)
            span *= 2
        cnt = jnp.minimum(w, frames_before + rows + 1).astype(F32)
        diff = acc[HIST_ROWS:, :] / cnt - cur
        outs.append(_dot(diff.astype(BF16), w_pool_ref[g]))
    return jnp.concatenate(outs, axis=-1) * pool_scale


def _mix_out(x, mix, gains_ref):
    x = x + _rmsnorm(mix, gains_ref[1:2, :])
    return x, _rmsnorm(x, gains_ref[2:3, :]).astype(BF16)


def _ffn_chunk(c, hn, w_ff1_ref, w_ff2_ref):
    chunk = slice(c * FF_CHUNK, (c + 1) * FF_CHUNK)
    act = jnp.square(jnp.maximum(_dot(hn, w_ff1_ref[:, chunk]), 0.0)).astype(BF16)
    return _dot(act, w_ff2_ref[chunk, :])


def _tail(x, ff, p, gains_ref, w_ple_ref, w_gate_ref):
    x = x + _rmsnorm(ff, gains_ref[3:4, :])
    gate = jax.nn.sigmoid(_dot(x.astype(BF16), w_gate_ref[...]))
    ple = gate * _dot(p.astype(BF16), w_ple_ref[...])
    return x + _rmsnorm(ple, gains_ref[4:5, :])


def _finish(x, mixed, p, gains_ref, w_out_ref, w_ff1_ref, w_ff2_ref, w_ple_ref, w_gate_ref):
    x, hn = _mix_out(x, _dot(mixed, w_out_ref[...]), gains_ref)
    ff = None
    for c in range(D_FF // FF_CHUNK):
        part = _ffn_chunk(c, hn, w_ff1_ref, w_ff2_ref)
        ff = part if ff is None else ff + part
    return _tail(x, ff, p, gains_ref, w_ple_ref, w_gate_ref)


def _mask_lane(h):
    return HEAD_DIM if h % 2 == 0 else 0


def _prompt_tile(n_tiles, n_total, s, x_ref, p_ref, table_ref, gains_ref, pscale_ref, w_in_ref, w_pool_ref,
                 w_out_ref, w_ff1_ref, w_ff2_ref, w_ple_ref, w_gate_ref,
                 y_ref, knew_ref, vnew_ref, unew_ref,
                 kext, vt, uext, xbuf, mixbuf, x1_s, ff_s, wq_s, s_scr, p_scr):
    t = lax.rem(jnp.minimum(s, n_total - 1), n_tiles)

    @pl.when(s == 0)
    def _():
        xbuf[...] = jnp.zeros_like(xbuf)
        mixbuf[...] = jnp.zeros_like(mixbuf)
        x1_s[...] = jnp.zeros_like(x1_s)
        ff_s[...] = jnp.zeros_like(ff_s)

    @pl.when(t == 0)
    def _():
        lane = lax.broadcasted_iota(jnp.int32, (LK, 2 * HEAD_DIM), 1)
        for h in range(N_HEADS):
            kext[h] = jnp.where(lane == _mask_lane(h), NEG_INF, 0.0).astype(BF16)
        vt[...] = jnp.zeros_like(vt)
        uext[0:HIST_ROWS, :] = jnp.zeros((HIST_ROWS, D_POOL), F32)

    mix = _dot(mixbuf[...], w_out_ref[...])

    y_ref[0] = _tail(x1_s[...], ff_s[...], p_ref[0], gains_ref, w_ple_ref, w_gate_ref)

    x_prev, hn = _mix_out(xbuf[...], mix, gains_ref)
    x1_s[...] = x_prev

    def ffn_up(c):
        return jnp.square(jnp.maximum(_dot(hn, w_ff1_ref[:, c * FF_CHUNK:(c + 1) * FF_CHUNK]), 0.0)).astype(BF16)

    def ffn_down(c):
        return _dot(acts[c], w_ff2_ref[c * FF_CHUNK:(c + 1) * FF_CHUNK, :])

    def ffn_pieces(*ops):
        def run():
            total = None
            for kind, c in ops:
                if kind == "u":
                    acts[c] = ffn_up(c)
                else:
                    part = ffn_down(c)
                    total = part if total is None else total + part
            return total
        return run

    q, k, v, u = _project(x_ref[0], gains_ref, w_in_ref)
    acts = {0: ffn_up(0), 1: ffn_up(1)}
    knew_ref[0] = k
    vnew_ref[0] = v
    unew_ref[0] = u[TQ - HIST_ROWS:, :]
    uext[HIST_ROWS:HIST_ROWS + TQ, :] = u
    q_t = q.T
    v_t = v.T.astype(BF16)
    row = lax.broadcasted_iota(jnp.int32, (2 * HEAD_DIM, TQ), 0)
    lane = lax.broadcasted_iota(jnp.int32, (TQ, 2 * HEAD_DIM), 1)
    ones_rows = jnp.where(lax.broadcasted_iota(jnp.int32, (V_ROWS - HEAD_DIM, TQ), 0) == 0, 1.0, 0.0).astype(BF16)
    for h in range(N_HEADS):
        lanes = slice(2 * HEAD_DIM * (h // 2), 2 * HEAD_DIM * (h // 2 + 1))
        own = (lane < HEAD_DIM) if h % 2 == 0 else (lane >= HEAD_DIM)
        kext[h, 0:LEFT, :] = kext[h, TQ:LK, :]
        kext[h, LEFT:LK, :] = jnp.where(own, k[:, lanes], 0.0).astype(BF16)
        wq_s[h] = jnp.where(row == _mask_lane(h), 1.0, q_t[lanes, :]).astype(BF16)
        vt[h, :, 0:LEFT] = vt[h, :, TQ:LK]
        vt[h, :, LEFT:LK] = jnp.concatenate([v_t[HEAD_DIM * h:HEAD_DIM * (h + 1), :], ones_rows], axis=0)

    def scores(h):
        half = LK // 2
        s_scr[h % 2, 0:half, :] = _dot(kext[h, 0:half, :], wq_s[h])
        s_scr[h % 2, half:LK, :] = _dot(kext[h, half:LK, :], wq_s[h])

    def values(h):
        o = _dot(vt[h], p_scr[h % 2])
        return o[0:HEAD_DIM, :] / o[HEAD_DIM:HEAD_DIM + 1, :]

    fillers = [ffn_pieces(("d", 0), ("u", 2)), ffn_pieces(("d", 1), ("u", 3)), ffn_pieces(("d", 2), ("u", 4)),
               ffn_pieces(("d", 3), ("u", 5)), ffn_pieces(("d", 4), ("u", 6)), ffn_pieces(("d", 5), ("u", 7)),
               ffn_pieces(("d", 6), ("d", 7))]
    scores(0)
    scores(1)
    ff = None
    heads = []
    for h in range(N_HEADS):
        sc = s_scr[h % 2] + table_ref[h]
        p_scr[h % 2] = jnp.exp2(sc - jnp.max(sc, axis=0, keepdims=True)).astype(BF16)
        if h < len(fillers):
            part = fillers[h]()
            ff = part if ff is None else ff + part
        heads.append(values(h))
        if h + 2 < N_HEADS:
            scores(h + 2)
        if h == POOL_ROUND:
            pool = _pool(uext, TQ, t * TQ, w_pool_ref, pscale_ref[...])
            uext[0:HIST_ROWS, :] = uext[TQ:TQ + HIST_ROWS, :]
            mixbuf[:, D_ATTN:] = pool.astype(BF16)
    ff_s[...] = ff
    mixbuf[:, 0:D_ATTN] = jnp.concatenate(heads, axis=0).T.astype(BF16)
    xbuf[...] = x_ref[0]


def _prompt_kernel(n_tiles, n_total, x_hbm, p_hbm, *refs):
    n_const = 10
    consts, (y_hbm, k_hbm, v_hbm, u_hbm) = refs[:n_const], refs[n_const:n_const + 4]
    *tile_scratch, xin, pin, yout, kbuf, vbuf, ubuf, sem_x, sem_p, sem_y, sem_o = refs[n_const + 4:]
    last = n_total - 1

    def x_copy(tile, slot):
        return pltpu.make_async_copy(x_hbm.at[tile], xin.at[slot], sem_x.at[slot])

    def p_copy(tile, slot):
        return pltpu.make_async_copy(p_hbm.at[tile], pin.at[slot], sem_p.at[slot])

    def y_copy(tile, slot):
        return pltpu.make_async_copy(yout.at[slot], y_hbm.at[tile], sem_y.at[slot])

    x_copy(0, 0).start()
    pin[...] = jnp.zeros_like(pin)

    def tile_step(s, carry):
        slot = lax.rem(s, 2)
        other = 1 - slot

        @pl.when(s <= last)
        def _():
            x_copy(s, slot).wait()

        @pl.when(s >= 2)
        def _():
            p_copy(s - 2, slot).wait()

        @pl.when(s >= 4)
        def _():
            y_copy(s - 4, slot).wait()

        @pl.when(s + 1 <= last)
        def _():
            x_copy(s + 1, other).start()

        @pl.when((s >= 1) & (s - 1 <= last))
        def _():
            p_copy(s - 1, other).start()

        _prompt_tile(n_tiles, n_total, s, xin.at[pl.ds(slot, 1)], pin.at[pl.ds(slot, 1)], *consts,
                     yout.at[pl.ds(slot, 1)], kbuf, vbuf, ubuf, *tile_scratch)

        @pl.when(s >= 2)
        def _():
            y_copy(s - 2, slot).start()

        t = lax.rem(s, n_tiles)
        seq = s // n_tiles

        @pl.when((s <= last) & (t >= n_tiles - LEFT // TQ))
        def _():
            rows = pl.ds(pl.multiple_of((t - (n_tiles - LEFT // TQ)) * TQ, TQ), TQ)
            k_out = pltpu.make_async_copy(kbuf.at[0], k_hbm.at[seq, rows], sem_o.at[0])
            v_out = pltpu.make_async_copy(vbuf.at[0], v_hbm.at[seq, rows], sem_o.at[1])
            k_out.start()
            v_out.start()
            k_out.wait()
            v_out.wait()

        @pl.when((s <= last) & (t == n_tiles - 1))
        def _():
            u_out = pltpu.make_async_copy(ubuf.at[0], u_hbm.at[seq], sem_o.at[2])
            u_out.start()
            u_out.wait()

        return carry

    lax.fori_loop(0, n_total + 2, tile_step, 0)
    y_copy(last - 1, lax.rem(n_total, 2)).wait()
    y_copy(last, lax.rem(n_total + 1, 2)).wait()


def _resident(a):
    return pl.BlockSpec(a.shape, lambda *_: (0,) * a.ndim, pipeline_mode=pl.Buffered(1))


def _prompt_layer(x, p, table, gains, pscale, weights):
    batch, seq, _ = x.shape
    assert seq % TQ == 0 and seq >= LEFT and LEFT % TQ == 0 and D_FF // FF_CHUNK == N_HEADS
    n_tiles = seq // TQ
    n_total = batch * n_tiles
    consts = (table, gains, pscale) + tuple(weights)
    in_hbm = pl.BlockSpec(memory_space=pl.ANY)
    in_vmem = pl.BlockSpec(memory_space=pltpu.VMEM)
    y, k_new, v_new, u_new = pl.pallas_call(
        functools.partial(_prompt_kernel, n_tiles, n_total),
        in_specs=[in_hbm, in_hbm] + [in_vmem] * len(consts),
        out_specs=[in_hbm] * 4,
        out_shape=[jax.ShapeDtypeStruct((n_total, TQ, D_MODEL), F32),
                   jax.ShapeDtypeStruct((batch, LEFT, D_ATTN), F32),
                   jax.ShapeDtypeStruct((batch, LEFT, D_ATTN), F32),
                   jax.ShapeDtypeStruct((batch, HIST_ROWS, D_POOL), F32)],
        scratch_shapes=[pltpu.VMEM((N_HEADS, LK, 2 * HEAD_DIM), BF16), pltpu.VMEM((N_HEADS, V_ROWS, LK), BF16),
                        pltpu.VMEM((HIST_ROWS + TQ, D_POOL), F32), pltpu.VMEM((TQ, D_MODEL), F32),
                        pltpu.VMEM((TQ, D_ATTN + D_POOL), BF16), pltpu.VMEM((TQ, D_MODEL), F32),
                        pltpu.VMEM((TQ, D_MODEL), F32), pltpu.VMEM((N_HEADS, 2 * HEAD_DIM, TQ), BF16),
                        pltpu.VMEM((2, LK, TQ), F32), pltpu.VMEM((2, LK, TQ), BF16),
                        pltpu.VMEM((2, TQ, D_MODEL), F32), pltpu.VMEM((2, TQ, D_PLE), F32),
                        pltpu.VMEM((2, TQ, D_MODEL), F32), pltpu.VMEM((1, TQ, D_ATTN), F32),
                        pltpu.VMEM((1, TQ, D_ATTN), F32), pltpu.VMEM((1, HIST_ROWS, D_POOL), F32),
                        pltpu.SemaphoreType.DMA((2,)), pltpu.SemaphoreType.DMA((2,)),
                        pltpu.SemaphoreType.DMA((2,)), pltpu.SemaphoreType.DMA((3,))],
        compiler_params=pltpu.CompilerParams(vmem_limit_bytes=VMEM_LIMIT_BYTES),
        name="prompt_layer",
    )(x.reshape(n_total, TQ, D_MODEL), p.reshape(n_total, TQ, D_PLE), *consts)
    return y.reshape(batch, seq, D_MODEL), k_new, v_new, u_new


def _sample_kernel(n_seq, x_ref, p_ref, ck_ref, cv_ref, sp_ref, table_ref, gains_ref, pscale_ref, w_in_ref,
                   w_pool_ref, w_out_ref, w_ff1_ref, w_ff2_ref, w_ple_ref, w_gate_ref,
                   y_ref, knew_ref, vnew_ref, unew_ref, q_s, mix_s, kext, vt, uext):
    b = pl.program_id(0)
    row0 = pl.multiple_of(b * n_seq, n_seq)

    @pl.when(b == 0)
    def _():
        q, k, v, u = _project(x_ref[...], gains_ref, w_in_ref)
        q_s[...] = q
        knew_ref[...] = k
        vnew_ref[...] = v
        mix_s[:, D_ATTN:] = u
        kext[LEFT:SAMPLE_LK, :] = jnp.zeros((SAMPLE_TQ, D_ATTN), BF16)
        vt[:, LEFT:SAMPLE_LK] = jnp.zeros((D_ATTN, SAMPLE_TQ), BF16)

    pad = jnp.zeros((SAMPLE_TQ - n_seq, D_ATTN), F32)
    q = jnp.concatenate([q_s[pl.ds(row0, n_seq), :], pad], axis=0)
    v = jnp.concatenate([vnew_ref[pl.ds(row0, n_seq), :], pad], axis=0)
    u = mix_s[pl.ds(row0, n_seq), D_ATTN:]
    kext[0:LEFT, :] = ck_ref[0].astype(BF16)
    kext[LEFT:LEFT + n_seq, :] = knew_ref[pl.ds(row0, n_seq), :].astype(BF16)
    vt[:, 0:LEFT] = cv_ref[0].T.astype(BF16)
    vt[:, LEFT:SAMPLE_LK] = v.T.astype(BF16)
    uext[0:HIST_ROWS, :] = sp_ref[0]
    uext[HIST_ROWS:HIST_ROWS + n_seq, :] = u
    unew_ref[0] = u[n_seq - HIST_ROWS:, :]

    kk = lax.broadcasted_iota(jnp.int32, (SAMPLE_LK, SAMPLE_TQ), 0)
    attn = _attention_t(q.T, kext, vt, table_ref, kk < LEFT + n_seq).T
    pool = _pool(uext, n_seq, POOL_HIST, w_pool_ref, pscale_ref[...])
    mix_s[pl.ds(row0, n_seq), 0:D_ATTN] = attn[0:n_seq, :]
    mix_s[pl.ds(row0, n_seq), D_ATTN:] = pool

    @pl.when(b == pl.num_programs(0) - 1)
    def _():
        y_ref[...] = _finish(x_ref[...], mix_s[...].astype(BF16), p_ref[...], gains_ref, w_out_ref, w_ff1_ref,
                             w_ff2_ref, w_ple_ref, w_gate_ref)


def _sample_layer(x, p, cache_k, cache_v, state_pool, table, gains, pscale, weights):
    batch, n_seq, _ = x.shape
    rows = batch * n_seq
    assert cache_k.shape[1] == LEFT and HIST_ROWS <= n_seq <= SAMPLE_TQ and n_seq % 8 == 0
    full = lambda shape: pl.BlockSpec(shape, lambda b: (0,) * len(shape))
    per_b = lambda shape: pl.BlockSpec((1,) + shape, lambda b: (b, 0, 0))
    return pl.pallas_call(
        functools.partial(_sample_kernel, n_seq),
        grid=(batch,),
        in_specs=[full((rows, D_MODEL)), full((rows, D_PLE)), per_b((LEFT, D_ATTN)), per_b((LEFT, D_ATTN)),
                  per_b((HIST_ROWS, D_POOL)), full((N_HEADS, SAMPLE_LK, SAMPLE_TQ))]
        + [_resident(a) for a in (gains, pscale) + tuple(weights)],
        out_specs=[full((rows, D_MODEL)), full((rows, D_ATTN)), full((rows, D_ATTN)), per_b((HIST_ROWS, D_POOL))],
        out_shape=[jax.ShapeDtypeStruct((rows, D_MODEL), F32), jax.ShapeDtypeStruct((rows, D_ATTN), F32),
                   jax.ShapeDtypeStruct((rows, D_ATTN), F32),
                   jax.ShapeDtypeStruct((batch, HIST_ROWS, D_POOL), F32)],
        scratch_shapes=[pltpu.VMEM((rows, D_ATTN), F32), pltpu.VMEM((rows, D_ATTN + D_POOL), F32),
                        pltpu.VMEM((SAMPLE_LK, D_ATTN), BF16), pltpu.VMEM((D_ATTN, SAMPLE_LK), BF16),
                        pltpu.VMEM((HIST_ROWS + n_seq, D_POOL), F32)],
        compiler_params=pltpu.CompilerParams(dimension_semantics=("arbitrary",),
                                             vmem_limit_bytes=VMEM_LIMIT_BYTES),
        name="sample_layer",
    )(x.reshape(rows, D_MODEL), p.reshape(rows, D_PLE), cache_k, cache_v, state_pool, table, gains, pscale,
      *weights)


def kernel(x_prompt, x_sample, cache_k, cache_v, state_pool, p_prompt, p_sample, g_mix_pre, g_mix_post,
           g_ff_pre, g_ff_post, g_ple_post, w_in, rel_bias, w_pool, pool_scale, w_out, w_ff1, w_ff2, w_ple,
           w_ple_gate):
    depth = w_in.shape[0]
    batch, seq, _ = x_prompt.shape
    dec_batch, dec_seq, _ = x_sample.shape
    xp, xs = x_prompt, x_sample
    outs = [[] for _ in range(6)]
    for i in range(depth):
        table = _bias_table(rel_bias[i])
        gains = jnp.concatenate(
            [g_mix_pre[i:i + 1], g_mix_post[i:i + 1], g_ff_pre[i:i + 1], g_ff_post[i:i + 1], g_ple_post[i:i + 1],
             jnp.zeros((3, D_MODEL), F32)], axis=0)
        pscale = pool_scale[i:i + 1]
        weights = tuple(w[i].astype(BF16) for w in (w_in, w_pool, w_out, w_ff1, w_ff2, w_ple, w_ple_gate))
        xp, kp, vp, up = _prompt_layer(xp, p_prompt[i], table, gains, pscale, weights)
        sp = jnp.pad(state_pool[i], ((0, 0), (HIST_ROWS - POOL_HIST, 0), (0, 0)))
        xs2, kn, vn, un = _sample_layer(xs, p_sample[i], cache_k[i].reshape(dec_batch, LEFT, D_ATTN),
                                        cache_v[i].reshape(dec_batch, LEFT, D_ATTN), sp, table, gains, pscale,
                                        weights)
        xs = xs2.reshape(dec_batch, dec_seq, D_MODEL)
        outs[0].append(kp.reshape(batch, LEFT, N_HEADS, HEAD_DIM))
        outs[1].append(vp.reshape(batch, LEFT, N_HEADS, HEAD_DIM))
        outs[2].append(up[:, HIST_ROWS - POOL_HIST:, :])
        outs[3].append(kn.reshape(dec_batch, dec_seq, N_HEADS, HEAD_DIM))
        outs[4].append(vn.reshape(dec_batch, dec_seq, N_HEADS, HEAD_DIM))
        outs[5].append(un[:, HIST_ROWS - POOL_HIST:, :])
    return (xp, xs) + tuple(jnp.stack(o) for o in outs)
```

```python
import functools

import jax
import jax.numpy as jnp
from jax import lax
from jax.experimental import pallas as pl
from jax.experimental.pallas import tpu as pltpu

D_MODEL = 1024
D_ATTN = 512
D_POOL = 512
HEAD_DIM = 64
N_HEADS = 8
CHUNK = 64
LEFT_CHUNKS = 8
LEFT = LEFT_CHUNKS * CHUNK
REL_CLIP = 128
N_REL = 2 * REL_CLIP + 1
POOL_WINDOWS = (2, 4, 8, 16)
POOL_GROUP_DIM = 128
POOL_HIST = 15
HIST_ROWS = 16
D_FF = 4096
FF_CHUNK = 512
D_PLE = 256
EPS = 1e-6
NEG_INF = -1e30
ATTN_SCALE = HEAD_DIM ** -0.5
LOG2E = 1.4426950408889634
V_ROWS = HEAD_DIM + 16

TQ = 256
LK = LEFT + TQ
SAMPLE_TQ = 128
SAMPLE_LK = LEFT + SAMPLE_TQ
TABLE_PAD = 256
V7X_VMEM_BYTES = 64 * 1024 * 1024
VMEM_LIMIT_BYTES = V7X_VMEM_BYTES * 7 // 8
POOL_ROUND = 2
PSCALE_ROWS = 8

F32 = jnp.float32
BF16 = jnp.bfloat16


def _bias_table_kernel(rbt_ref, out_ref, buf_a, buf_b):
    far = jnp.broadcast_to(rbt_ref[0, 0:1, :], (TABLE_PAD + LEFT - REL_CLIP, TQ))
    buf_a[0:TABLE_PAD + LEFT - REL_CLIP, :] = far
    buf_b[0:TABLE_PAD, :] = far[0:TABLE_PAD]
    buf_a[TABLE_PAD + LEFT - REL_CLIP:TABLE_PAD + LEFT + REL_CLIP, :] = jnp.broadcast_to(
        rbt_ref[0, 0:2 * REL_CLIP, :], (2 * REL_CLIP, TQ))
    buf_a[TABLE_PAD + LEFT + REL_CLIP:TABLE_PAD + LK, :] = jnp.broadcast_to(
        rbt_ref[0, 2 * REL_CLIP:2 * REL_CLIP + 1, :], (LK - LEFT - REL_CLIP, TQ))

    rows = 128
    qidx = lax.broadcasted_iota(jnp.int32, (rows, TQ), 1)
    src, dst = buf_a, buf_b
    for b in range(TQ.bit_length() - 1):
        s = 1 << b
        bit = (qidx & s) != 0
        for r0 in range(TABLE_PAD, TABLE_PAD + LK, rows):
            dst[r0:r0 + rows, :] = jnp.where(bit, src[r0 - s:r0 - s + rows, :], src[r0:r0 + rows, :])
        src, dst = dst, src

    kk = lax.broadcasted_iota(jnp.int32, (LK, TQ), 0)
    qq = lax.broadcasted_iota(jnp.int32, (LK, TQ), 1)
    d = (kk >> 6) - (qq >> 6)
    out_ref[0] = jnp.where((d >= 0) & (d <= LEFT_CHUNKS), src[TABLE_PAD:TABLE_PAD + LK, :] * LOG2E, NEG_INF)


def _bias_table(rel_bias):
    rbt = jnp.pad(rel_bias[:, ::-1], ((0, 0), (0, 384 - N_REL)))[:, :, None]
    return pl.pallas_call(
        _bias_table_kernel,
        grid=(N_HEADS,),
        in_specs=[pl.BlockSpec((1, 384, 1), lambda h: (h, 0, 0))],
        out_specs=pl.BlockSpec((1, LK, TQ), lambda h: (h, 0, 0)),
        out_shape=jax.ShapeDtypeStruct((N_HEADS, LK, TQ), F32),
        scratch_shapes=[pltpu.VMEM((TABLE_PAD + LK, TQ), F32), pltpu.VMEM((TABLE_PAD + LK, TQ), F32)],
        name="bias_table",
    )(rbt)


def _rmsnorm(x, g):
    y = x * lax.rsqrt(jnp.mean(x * x, axis=-1, keepdims=True) + EPS)
    return y * g


def _dot(a, b):
    return jnp.dot(a, b, preferred_element_type=F32)


def _project(x, gains_ref, w_in_ref):
    h = _rmsnorm(x, gains_ref[0:1, :]).astype(BF16)
    qkvu = _dot(h, w_in_ref[...])
    q = qkvu[:, 0:D_ATTN] * (ATTN_SCALE * LOG2E)
    k = qkvu[:, D_ATTN:2 * D_ATTN]
    v = qkvu[:, 2 * D_ATTN:3 * D_ATTN]
    u = qkvu[:, 3 * D_ATTN:]
    return q, k, v, u


def _attention_t(q_t, kext_ref, vt_ref, table_ref, valid):
    nq = q_t.shape[1]
    row = lax.broadcasted_iota(jnp.int32, (2 * HEAD_DIM, nq), 0)
    scores = []
    for h in range(N_HEADS):
        pair = slice(2 * HEAD_DIM * (h // 2), 2 * HEAD_DIM * (h // 2 + 1))
        own = (row < HEAD_DIM) if h % 2 == 0 else (row >= HEAD_DIM)
        scores.append(_dot(kext_ref[:, pair], jnp.where(own, q_t[pair, :], 0.0).astype(BF16)))
    probs, denoms = [], []
    for h in range(N_HEADS):
        s = jnp.where(valid, scores[h] + table_ref[h], NEG_INF)
        p = jnp.exp2(s - jnp.max(s, axis=0, keepdims=True))
        denoms.append(jnp.sum(p, axis=0, keepdims=True))
        probs.append(p.astype(BF16))
    outs = [_dot(vt_ref[HEAD_DIM * h:HEAD_DIM * (h + 1), :], probs[h]) / denoms[h] for h in range(N_HEADS)]
    return jnp.concatenate(outs, axis=0)


def _pool(uext_ref, n_rows, frames_before, w_pool_ref, pool_scale):
    rows = lax.broadcasted_iota(jnp.int32, (n_rows, POOL_GROUP_DIM), 0)
    outs = []
    for g, w in enumerate(POOL_WINDOWS):
        lo = g * POOL_GROUP_DIM
        acc = uext_ref[0:HIST_ROWS + n_rows, lo:lo + POOL_GROUP_DIM]
        cur = acc[HIST_ROWS:, :]
        span = 1
        while span < w:
            acc = acc + pltpu.roll(acc, span, 0)
            span *= 2
        cnt = jnp.minimum(w, frames_before + rows + 1).astype(F32)
        diff = acc[HIST_ROWS:, :] / cnt - cur
        outs.append(_dot(diff.astype(BF16), w_pool_ref[g]))
    return jnp.concatenate(outs, axis=-1) * pool_scale


def _mix_out(x, mix, gains_ref):
    x = x + _rmsnorm(mix, gains_ref[1:2, :])
    return x, _rmsnorm(x, gains_ref[2:3, :]).astype(BF16)


def _ffn_chunk(c, hn, w_ff1_ref, w_ff2_ref):
    chunk = slice(c * FF_CHUNK, (c + 1) * FF_CHUNK)
    act = jnp.square(jnp.maximum(_dot(hn, w_ff1_ref[:, chunk]), 0.0)).astype(BF16)
    return _dot(act, w_ff2_ref[chunk, :])


def _tail(x, ff, p, gains_ref, w_ple_ref, w_gate_ref):
    x = x + _rmsnorm(ff, gains_ref[3:4, :])
    gate = jax.nn.sigmoid(_dot(x.astype(BF16), w_gate_ref[...]))
    ple = gate * _dot(p.astype(BF16), w_ple_ref[...])
    return x + _rmsnorm(ple, gains_ref[4:5, :])


def _finish(x, mixed, p, gains_ref, w_out_ref, w_ff1_ref, w_ff2_ref, w_ple_ref, w_gate_ref):
    x, hn = _mix_out(x, _dot(mixed, w_out_ref[...]), gains_ref)
    ff = None
    for c in range(D_FF // FF_CHUNK):
        part = _ffn_chunk(c, hn, w_ff1_ref, w_ff2_ref)
        ff = part if ff is None else ff + part
    return _tail(x, ff, p, gains_ref, w_ple_ref, w_gate_ref)


def _mask_lane(h):
    return HEAD_DIM if h % 2 == 0 else 0


def _prompt_tile(n_tiles, n_total, s, x_ref, p_ref, table_ref, gains_ref, pscale_ref, w_in_ref, w_pool_ref,
                 w_out_ref, w_ff1_ref, w_ff2_ref, w_ple_ref, w_gate_ref,
                 y_ref, knew_ref, vnew_ref, unew_ref,
                 kext, vt, uext, xbuf, mixbuf, x1_s, ff_s, wq_s, s_scr, p_scr):
    t = lax.rem(jnp.minimum(s, n_total - 1), n_tiles)

    @pl.when(s == 0)
    def _():
        xbuf[...] = jnp.zeros_like(xbuf)
        mixbuf[...] = jnp.zeros_like(mixbuf)
        x1_s[...] = jnp.zeros_like(x1_s)
        ff_s[...] = jnp.zeros_like(ff_s)

    @pl.when(t == 0)
    def _():
        lane = lax.broadcasted_iota(jnp.int32, (LK, 2 * HEAD_DIM), 1)
        for h in range(N_HEADS):
            kext[h] = jnp.where(lane == _mask_lane(h), NEG_INF, 0.0).astype(BF16)
        vt[...] = jnp.zeros_like(vt)
        uext[0:HIST_ROWS, :] = jnp.zeros((HIST_ROWS, D_POOL), F32)

    mix = _dot(mixbuf[...], w_out_ref[...])

    y_ref[0] = _tail(x1_s[...], ff_s[...], p_ref[0], gains_ref, w_ple_ref, w_gate_ref)

    x_prev, hn = _mix_out(xbuf[...], mix, gains_ref)
    x1_s[...] = x_prev

    def ffn_up(c):
        return jnp.square(jnp.maximum(_dot(hn, w_ff1_ref[:, c * FF_CHUNK:(c + 1) * FF_CHUNK]), 0.0)).astype(BF16)

    def ffn_down(c):
        return _dot(acts[c], w_ff2_ref[c * FF_CHUNK:(c + 1) * FF_CHUNK, :])

    def ffn_pieces(*ops):
        def run():
            total = None
            for kind, c in ops:
                if kind == "u":
                    acts[c] = ffn_up(c)
                else:
                    part = ffn_down(c)
                    total = part if total is None else total + part
            return total
        return run

    q, k, v, u = _project(x_ref[0], gains_ref, w_in_ref)
    acts = {0: ffn_up(0), 1: ffn_up(1)}
    knew_ref[0] = k
    vnew_ref[0] = v
    unew_ref[0] = u[TQ - HIST_ROWS:, :]
    uext[HIST_ROWS:HIST_ROWS + TQ, :] = u
    q_t = q.T
    v_t = v.T.astype(BF16)
    row = lax.broadcasted_iota(jnp.int32, (2 * HEAD_DIM, TQ), 0)
    lane = lax.broadcasted_iota(jnp.int32, (TQ, 2 * HEAD_DIM), 1)
    ones_rows = jnp.where(lax.broadcasted_iota(jnp.int32, (V_ROWS - HEAD_DIM, TQ), 0) == 0, 1.0, 0.0).astype(BF16)
    for h in range(N_HEADS):
        lanes = slice(2 * HEAD_DIM * (h // 2), 2 * HEAD_DIM * (h // 2 + 1))
        own = (lane < HEAD_DIM) if h % 2 == 0 else (lane >= HEAD_DIM)
        kext[h, 0:LEFT, :] = kext[h, TQ:LK, :]
        kext[h, LEFT:LK, :] = jnp.where(own, k[:, lanes], 0.0).astype(BF16)
        wq_s[h] = jnp.where(row == _mask_lane(h), 1.0, q_t[lanes, :]).astype(BF16)
        vt[h, :, 0:LEFT] = vt[h, :, TQ:LK]
        vt[h, :, LEFT:LK] = jnp.concatenate([v_t[HEAD_DIM * h:HEAD_DIM * (h + 1), :], ones_rows], axis=0)

    def scores(h):
        half = LK // 2
        s_scr[h % 2, 0:half, :] = _dot(kext[h, 0:half, :], wq_s[h])
        s_scr[h % 2, half:LK, :] = _dot(kext[h, half:LK, :], wq_s[h])

    def values(h):
        o = _dot(vt[h], p_scr[h % 2])
        return o[0:HEAD_DIM, :] / o[HEAD_DIM:HEAD_DIM + 1, :]

    fillers = [ffn_pieces(("d", 0), ("u", 2)), ffn_pieces(("d", 1), ("u", 3)), ffn_pieces(("d", 2), ("u", 4)),
               ffn_pieces(("d", 3), ("u", 5)), ffn_pieces(("d", 4), ("u", 6)), ffn_pieces(("d", 5), ("u", 7)),
               ffn_pieces(("d", 6), ("d", 7))]
    scores(0)
    scores(1)
    ff = None
    heads = []
    for h in range(N_HEADS):
        sc = s_scr[h % 2] + table_ref[h]
        p_scr[h % 2] = jnp.exp2(sc - jnp.max(sc, axis=0, keepdims=True)).astype(BF16)
        if h < len(fillers):
            part = fillers[h]()
            ff = part if ff is None else ff + part
        heads.append(values(h))
        if h + 2 < N_HEADS:
            scores(h + 2)
        if h == POOL_ROUND:
            pool = _pool(uext, TQ, t * TQ, w_pool_ref, pscale_ref[0:1, :])
            uext[0:HIST_ROWS, :] = uext[TQ:TQ + HIST_ROWS, :]
            mixbuf[:, D_ATTN:] = pool.astype(BF16)
    ff_s[...] = ff
    mixbuf[:, 0:D_ATTN] = jnp.concatenate(heads, axis=0).T.astype(BF16)
    xbuf[...] = x_ref[0]


def _prompt_kernel(n_tiles, n_total, x_hbm, p_hbm, *refs):
    n_const = 10
    consts, (y_hbm, k_hbm, v_hbm, u_hbm) = refs[:n_const], refs[n_const:n_const + 4]
    *tile_scratch, xin, pin, yout, kbuf, vbuf, ubuf, sem_x, sem_p, sem_y, sem_o = refs[n_const + 4:]
    last = n_total - 1

    def x_copy(tile, slot):
        return pltpu.make_async_copy(x_hbm.at[tile], xin.at[slot], sem_x.at[slot])

    def p_copy(tile, slot):
        return pltpu.make_async_copy(p_hbm.at[tile], pin.at[slot], sem_p.at[slot])

    def y_copy(tile, slot):
        return pltpu.make_async_copy(yout.at[slot], y_hbm.at[tile], sem_y.at[slot])

    x_copy(0, 0).start()
    pin[...] = jnp.zeros_like(pin)

    def tile_step(s, carry):
        slot = lax.rem(s, 2)
        other = 1 - slot

        @pl.when(s <= last)
        def _():
            x_copy(s, slot).wait()

        @pl.when(s >= 2)
        def _():
            p_copy(s - 2, slot).wait()

        @pl.when(s >= 4)
        def _():
            y_copy(s - 4, slot).wait()

        @pl.when(s + 1 <= last)
        def _():
            x_copy(s + 1, other).start()

        @pl.when((s >= 1) & (s - 1 <= last))
        def _():
            p_copy(s - 1, other).start()

        _prompt_tile(n_tiles, n_total, s, xin.at[pl.ds(slot, 1)], pin.at[pl.ds(slot, 1)], *consts,
                     yout.at[pl.ds(slot, 1)], kbuf, vbuf, ubuf, *tile_scratch)

        @pl.when(s >= 2)
        def _():
            y_copy(s - 2, slot).start()

        t = lax.rem(s, n_tiles)
        seq = s // n_tiles

        @pl.when((s <= last) & (t >= n_tiles - LEFT // TQ))
        def _():
            rows = pl.ds(pl.multiple_of((t - (n_tiles - LEFT // TQ)) * TQ, TQ), TQ)
            k_out = pltpu.make_async_copy(kbuf.at[0], k_hbm.at[seq, rows], sem_o.at[0])
            v_out = pltpu.make_async_copy(vbuf.at[0], v_hbm.at[seq, rows], sem_o.at[1])
            k_out.start()
            v_out.start()
            k_out.wait()
            v_out.wait()

        @pl.when((s <= last) & (t == n_tiles - 1))
        def _():
            u_out = pltpu.make_async_copy(ubuf.at[0], u_hbm.at[seq], sem_o.at[2])
            u_out.start()
            u_out.wait()

        return carry

    lax.fori_loop(0, n_total + 2, tile_step, 0)
    y_copy(last - 1, lax.rem(n_total, 2)).wait()
    y_copy(last, lax.rem(n_total + 1, 2)).wait()


def _resident(a):
    return pl.BlockSpec(a.shape, lambda *_: (0,) * a.ndim, pipeline_mode=pl.Buffered(1))


def _prompt_layer(x, p, table, gains, pscale, weights):
    batch, seq, _ = x.shape
    assert seq % TQ == 0 and seq >= LEFT and LEFT % TQ == 0 and D_FF // FF_CHUNK == N_HEADS
    n_tiles = seq // TQ
    n_total = batch * n_tiles
    consts = (table, gains, pscale) + tuple(weights)
    in_hbm = pl.BlockSpec(memory_space=pl.ANY)
    in_vmem = pl.BlockSpec(memory_space=pltpu.VMEM)
    y, k_new, v_new, u_new = pl.pallas_call(
        functools.partial(_prompt_kernel, n_tiles, n_total),
        in_specs=[in_hbm, in_hbm] + [in_vmem] * len(consts),
        out_specs=[in_hbm] * 4,
        out_shape=[jax.ShapeDtypeStruct((n_total, TQ, D_MODEL), F32),
                   jax.ShapeDtypeStruct((batch, LEFT, D_ATTN), F32),
                   jax.ShapeDtypeStruct((batch, LEFT, D_ATTN), F32),
                   jax.ShapeDtypeStruct((batch, HIST_ROWS, D_POOL), F32)],
        scratch_shapes=[pltpu.VMEM((N_HEADS, LK, 2 * HEAD_DIM), BF16), pltpu.VMEM((N_HEADS, V_ROWS, LK), BF16),
                        pltpu.VMEM((HIST_ROWS + TQ, D_POOL), F32), pltpu.VMEM((TQ, D_MODEL), F32),
                        pltpu.VMEM((TQ, D_ATTN + D_POOL), BF16), pltpu.VMEM((TQ, D_MODEL), F32),
                        pltpu.VMEM((TQ, D_MODEL), F32), pltpu.VMEM((N_HEADS, 2 * HEAD_DIM, TQ), BF16),
                        pltpu.VMEM((2, LK, TQ), F32), pltpu.VMEM((2, LK, TQ), BF16),
                        pltpu.VMEM((2, TQ, D_MODEL), F32), pltpu.VMEM((2, TQ, D_PLE), F32),
                        pltpu.VMEM((2, TQ, D_MODEL), F32), pltpu.VMEM((1, TQ, D_ATTN), F32),
                        pltpu.VMEM((1, TQ, D_ATTN), F32), pltpu.VMEM((1, HIST_ROWS, D_POOL), F32),
                        pltpu.SemaphoreType.DMA((2,)), pltpu.SemaphoreType.DMA((2,)),
                        pltpu.SemaphoreType.DMA((2,)), pltpu.SemaphoreType.DMA((3,))],
        compiler_params=pltpu.CompilerParams(vmem_limit_bytes=VMEM_LIMIT_BYTES),
        name="prompt_layer",
    )(x.reshape(n_total, TQ, D_MODEL), p.reshape(n_total, TQ, D_PLE), *consts)
    return y.reshape(batch, seq, D_MODEL), k_new, v_new, u_new


def _sample_kernel(n_seq, x_ref, p_ref, ck_ref, cv_ref, sp_ref, table_ref, gains_ref, pscale_ref, w_in_ref,
                   w_pool_ref, w_out_ref, w_ff1_ref, w_ff2_ref, w_ple_ref, w_gate_ref,
                   y_ref, knew_ref, vnew_ref, unew_ref, q_s, mix_s, kext, vt, uext):
    b = pl.program_id(0)
    row0 = pl.multiple_of(b * n_seq, n_seq)

    @pl.when(b == 0)
    def _():
        q, k, v, u = _project(x_ref[...], gains_ref, w_in_ref)
        q_s[...] = q
        knew_ref[...] = k
        vnew_ref[...] = v
        mix_s[:, D_ATTN:] = u
        kext[LEFT:SAMPLE_LK, :] = jnp.zeros((SAMPLE_TQ, D_ATTN), BF16)
        vt[:, LEFT:SAMPLE_LK] = jnp.zeros((D_ATTN, SAMPLE_TQ), BF16)

    pad = jnp.zeros((SAMPLE_TQ - n_seq, D_ATTN), F32)
    q = jnp.concatenate([q_s[pl.ds(row0, n_seq), :], pad], axis=0)
    v = jnp.concatenate([vnew_ref[pl.ds(row0, n_seq), :], pad], axis=0)
    u = mix_s[pl.ds(row0, n_seq), D_ATTN:]
    kext[0:LEFT, :] = ck_ref[0].astype(BF16)
    kext[LEFT:LEFT + n_seq, :] = knew_ref[pl.ds(row0, n_seq), :].astype(BF16)
    vt[:, 0:LEFT] = cv_ref[0].T.astype(BF16)
    vt[:, LEFT:SAMPLE_LK] = v.T.astype(BF16)
    uext[0:HIST_ROWS, :] = sp_ref[0]
    uext[HIST_ROWS:HIST_ROWS + n_seq, :] = u
    unew_ref[0] = u[n_seq - HIST_ROWS:, :]

    kk = lax.broadcasted_iota(jnp.int32, (SAMPLE_LK, SAMPLE_TQ), 0)
    attn = _attention_t(q.T, kext, vt, table_ref, kk < LEFT + n_seq).T
    pool = _pool(uext, n_seq, POOL_HIST, w_pool_ref, pscale_ref[0:1, :])
    mix_s[pl.ds(row0, n_seq), 0:D_ATTN] = attn[0:n_seq, :]
    mix_s[pl.ds(row0, n_seq), D_ATTN:] = pool

    @pl.when(b == pl.num_programs(0) - 1)
    def _():
        y_ref[...] = _finish(x_ref[...], mix_s[...].astype(BF16), p_ref[...], gains_ref, w_out_ref, w_ff1_ref,
                             w_ff2_ref, w_ple_ref, w_gate_ref)


def _sample_layer(x, p, cache_k, cache_v, state_pool, table, gains, pscale, weights):
    batch, n_seq, _ = x.shape
    rows = batch * n_seq
    assert cache_k.shape[1] == LEFT and HIST_ROWS <= n_seq <= SAMPLE_TQ and n_seq % 8 == 0
    full = lambda shape: pl.BlockSpec(shape, lambda b: (0,) * len(shape))
    per_b = lambda shape: pl.BlockSpec((1,) + shape, lambda b: (b, 0, 0))
    return pl.pallas_call(
        functools.partial(_sample_kernel, n_seq),
        grid=(batch,),
        in_specs=[full((rows, D_MODEL)), full((rows, D_PLE)), per_b((LEFT, D_ATTN)), per_b((LEFT, D_ATTN)),
                  per_b((HIST_ROWS, D_POOL)), full((N_HEADS, SAMPLE_LK, SAMPLE_TQ))]
        + [_resident(a) for a in (gains, pscale) + tuple(weights)],
        out_specs=[full((rows, D_MODEL)), full((rows, D_ATTN)), full((rows, D_ATTN)), per_b((HIST_ROWS, D_POOL))],
        out_shape=[jax.ShapeDtypeStruct((rows, D_MODEL), F32), jax.ShapeDtypeStruct((rows, D_ATTN), F32),
                   jax.ShapeDtypeStruct((rows, D_ATTN), F32),
                   jax.ShapeDtypeStruct((batch, HIST_ROWS, D_POOL), F32)],
        scratch_shapes=[pltpu.VMEM((rows, D_ATTN), F32), pltpu.VMEM((rows, D_ATTN + D_POOL), F32),
                        pltpu.VMEM((SAMPLE_LK, D_ATTN), BF16), pltpu.VMEM((D_ATTN, SAMPLE_LK), BF16),
                        pltpu.VMEM((HIST_ROWS + n_seq, D_POOL), F32)],
        compiler_params=pltpu.CompilerParams(dimension_semantics=("arbitrary",),
                                             vmem_limit_bytes=VMEM_LIMIT_BYTES),
        name="sample_layer",
    )(x.reshape(rows, D_MODEL), p.reshape(rows, D_PLE), cache_k, cache_v, state_pool, table, gains, pscale,
      *weights)


def kernel(x_prompt, x_sample, cache_k, cache_v, state_pool, p_prompt, p_sample, g_mix_pre, g_mix_post,
           g_ff_pre, g_ff_post, g_ple_post, w_in, rel_bias, w_pool, pool_scale, w_out, w_ff1, w_ff2, w_ple,
           w_ple_gate):
    depth = w_in.shape[0]
    batch, seq, _ = x_prompt.shape
    dec_batch, dec_seq, _ = x_sample.shape
    xp, xs = x_prompt, x_sample
    outs = [[] for _ in range(6)]
    for i in range(depth):
        table = _bias_table(rel_bias[i])
        gains = jnp.concatenate(
            [g_mix_pre[i:i + 1], g_mix_post[i:i + 1], g_ff_pre[i:i + 1], g_ff_post[i:i + 1], g_ple_post[i:i + 1],
             jnp.zeros((3, D_MODEL), F32)], axis=0)
        pscale = jnp.broadcast_to(pool_scale[i:i + 1], (PSCALE_ROWS, D_POOL))
        weights = tuple(w[i].astype(BF16) for w in (w_in, w_pool, w_out, w_ff1, w_ff2, w_ple, w_ple_gate))
        xp, kp, vp, up = _prompt_layer(xp, p_prompt[i], table, gains, pscale, weights)
        sp = jnp.pad(state_pool[i], ((0, 0), (HIST_ROWS - POOL_HIST, 0), (0, 0)))
        xs2, kn, vn, un = _sample_layer(xs, p_sample[i], cache_k[i].reshape(dec_batch, LEFT, D_ATTN),
                                        cache_v[i].reshape(dec_batch, LEFT, D_ATTN), sp, table, gains, pscale,
                                        weights)
        xs = xs2.reshape(dec_batch, dec_seq, D_MODEL)
        outs[0].append(kp.reshape(batch, LEFT, N_HEADS, HEAD_DIM))
        outs[1].append(vp.reshape(batch, LEFT, N_HEADS, HEAD_DIM))
        outs[2].append(up[:, HIST_ROWS - POOL_HIST:, :])
        outs[3].append(kn.reshape(dec_batch, dec_seq, N_HEADS, HEAD_DIM))
        outs[4].append(vn.reshape(dec_batch, dec_seq, N_HEADS, HEAD_DIM))
        outs[5].append(un[:, HIST_ROWS - POOL_HIST:, :])
    return (xp, xs) + tuple(jnp.stack(o) for o in outs)
```

```python
import functools

import jax
import jax.numpy as jnp
from jax import lax
from jax.experimental import pallas as pl
from jax.experimental.pallas import tpu as pltpu

D_MODEL = 1024
D_ATTN = 512
D_POOL = 512
HEAD_DIM = 64
N_HEADS = 8
CHUNK = 64
LEFT_CHUNKS = 8
LEFT = LEFT_CHUNKS * CHUNK
REL_CLIP = 128
N_REL = 2 * REL_CLIP + 1
POOL_WINDOWS = (2, 4, 8, 16)
POOL_GROUP_DIM = 128
POOL_HIST = 15
HIST_ROWS = 16
D_FF = 4096
FF_CHUNK = 512
D_PLE = 256
EPS = 1e-6
NEG_INF = -1e30
ATTN_SCALE = HEAD_DIM ** -0.5
LOG2E = 1.4426950408889634
V_ROWS = HEAD_DIM + 16

TQ = 256
LK = LEFT + TQ
SAMPLE_TQ = 128
SAMPLE_LK = LEFT + SAMPLE_TQ
TABLE_PAD = 256
V7X_VMEM_BYTES = 64 * 1024 * 1024
VMEM_LIMIT_BYTES = V7X_VMEM_BYTES * 7 // 8
POOL_ROUND = 2
PSCALE_ROWS = 8

F32 = jnp.float32
BF16 = jnp.bfloat16


def _bias_table_kernel(rbt_ref, out_ref, buf_a, buf_b):
    far = jnp.broadcast_to(rbt_ref[0, 0:1, :], (TABLE_PAD + LEFT - REL_CLIP, TQ))
    buf_a[0:TABLE_PAD + LEFT - REL_CLIP, :] = far
    buf_b[0:TABLE_PAD, :] = far[0:TABLE_PAD]
    buf_a[TABLE_PAD + LEFT - REL_CLIP:TABLE_PAD + LEFT + REL_CLIP, :] = jnp.broadcast_to(
        rbt_ref[0, 0:2 * REL_CLIP, :], (2 * REL_CLIP, TQ))
    buf_a[TABLE_PAD + LEFT + REL_CLIP:TABLE_PAD + LK, :] = jnp.broadcast_to(
        rbt_ref[0, 2 * REL_CLIP:2 * REL_CLIP + 1, :], (LK - LEFT - REL_CLIP, TQ))

    rows = 128
    qidx = lax.broadcasted_iota(jnp.int32, (rows, TQ), 1)
    src, dst = buf_a, buf_b
    for b in range(TQ.bit_length() - 1):
        s = 1 << b
        bit = (qidx & s) != 0
        for r0 in range(TABLE_PAD, TABLE_PAD + LK, rows):
            dst[r0:r0 + rows, :] = jnp.where(bit, src[r0 - s:r0 - s + rows, :], src[r0:r0 + rows, :])
        src, dst = dst, src

    kk = lax.broadcasted_iota(jnp.int32, (LK, TQ), 0)
    qq = lax.broadcasted_iota(jnp.int32, (LK, TQ), 1)
    d = (kk >> 6) - (qq >> 6)
    out_ref[0] = jnp.where((d >= 0) & (d <= LEFT_CHUNKS), src[TABLE_PAD:TABLE_PAD + LK, :] * LOG2E, NEG_INF)


def _bias_table(rel_bias):
    rbt = jnp.pad(rel_bias[:, ::-1], ((0, 0), (0, 384 - N_REL)))[:, :, None]
    return pl.pallas_call(
        _bias_table_kernel,
        grid=(N_HEADS,),
        in_specs=[pl.BlockSpec((1, 384, 1), lambda h: (h, 0, 0))],
        out_specs=pl.BlockSpec((1, LK, TQ), lambda h: (h, 0, 0)),
        out_shape=jax.ShapeDtypeStruct((N_HEADS, LK, TQ), F32),
        scratch_shapes=[pltpu.VMEM((TABLE_PAD + LK, TQ), F32), pltpu.VMEM((TABLE_PAD + LK, TQ), F32)],
        name="bias_table",
    )(rbt)


def _rmsnorm(x, g):
    y = x * lax.rsqrt(jnp.mean(x * x, axis=-1, keepdims=True) + EPS)
    return y * g


def _dot(a, b):
    return jnp.dot(a, b, preferred_element_type=F32)


def _project(x, gains_ref, w_in_ref):
    h = _rmsnorm(x, gains_ref[0:1, :]).astype(BF16)
    qkvu = _dot(h, w_in_ref[...])
    q = qkvu[:, 0:D_ATTN] * (ATTN_SCALE * LOG2E)
    k = qkvu[:, D_ATTN:2 * D_ATTN]
    v = qkvu[:, 2 * D_ATTN:3 * D_ATTN]
    u = qkvu[:, 3 * D_ATTN:]
    return q, k, v, u


def _attention_t(q_t, kext_ref, vt_ref, table_ref, valid):
    nq = q_t.shape[1]
    row = lax.broadcasted_iota(jnp.int32, (2 * HEAD_DIM, nq), 0)
    scores = []
    for h in range(N_HEADS):
        pair = slice(2 * HEAD_DIM * (h // 2), 2 * HEAD_DIM * (h // 2 + 1))
        own = (row < HEAD_DIM) if h % 2 == 0 else (row >= HEAD_DIM)
        scores.append(_dot(kext_ref[:, pair], jnp.where(own, q_t[pair, :], 0.0).astype(BF16)))
    probs, denoms = [], []
    for h in range(N_HEADS):
        s = jnp.where(valid, scores[h] + table_ref[h], NEG_INF)
        p = jnp.exp2(s - jnp.max(s, axis=0, keepdims=True))
        denoms.append(jnp.sum(p, axis=0, keepdims=True))
        probs.append(p.astype(BF16))
    outs = [_dot(vt_ref[HEAD_DIM * h:HEAD_DIM * (h + 1), :], probs[h]) / denoms[h] for h in range(N_HEADS)]
    return jnp.concatenate(outs, axis=0)


def _pool(uext_ref, n_rows, frames_before, w_pool_ref, pool_scale):
    rows = lax.broadcasted_iota(jnp.int32, (n_rows, POOL_GROUP_DIM), 0)
    outs = []
    for g, w in enumerate(POOL_WINDOWS):
        lo = g * POOL_GROUP_DIM
        acc = uext_ref[0:HIST_ROWS + n_rows, lo:lo + POOL_GROUP_DIM]
        cur = acc[HIST_ROWS:, :]
        span = 1
        while span < w:
            acc = acc + pltpu.roll(acc, span, 0)
            span *= 2
        cnt = jnp.minimum(w, frames_before + rows + 1).astype(F32)
        diff = acc[HIST_ROWS:, :] / cnt - cur
        outs.append(_dot(diff.astype(BF16), w_pool_ref[g]))
    return jnp.concatenate(outs, axis=-1) * pool_scale


def _mix_out(x, mix, gains_ref):
    x = x + _rmsnorm(mix, gains_ref[1:2, :])
    return x, _rmsnorm(x, gains_ref[2:3, :]).astype(BF16)


def _ffn_chunk(c, hn, w_ff1_ref, w_ff2_ref):
    chunk = slice(c * FF_CHUNK, (c + 1) * FF_CHUNK)
    act = jnp.square(jnp.maximum(_dot(hn, w_ff1_ref[:, chunk]), 0.0)).astype(BF16)
    return _dot(act, w_ff2_ref[chunk, :])


def _tail(x, ff, p, gains_ref, w_ple_ref, w_gate_ref):
    x = x + _rmsnorm(ff, gains_ref[3:4, :])
    gate = jax.nn.sigmoid(_dot(x.astype(BF16), w_gate_ref[...]))
    ple = gate * _dot(p.astype(BF16), w_ple_ref[...])
    return x + _rmsnorm(ple, gains_ref[4:5, :])


def _finish(x, mixed, p, gains_ref, w_out_ref, w_ff1_ref, w_ff2_ref, w_ple_ref, w_gate_ref):
    x, hn = _mix_out(x, _dot(mixed, w_out_ref[...]), gains_ref)
    ff = None
    for c in range(D_FF // FF_CHUNK):
        part = _ffn_chunk(c, hn, w_ff1_ref, w_ff2_ref)
        ff = part if ff is None else ff + part
    return _tail(x, ff, p, gains_ref, w_ple_ref, w_gate_ref)


def _mask_lane(h):
    return HEAD_DIM if h % 2 == 0 else 0


def _prompt_tile(n_tiles, n_total, s, x_ref, p_ref, table_ref, gains_ref, pscale_ref, w_in_ref, w_pool_ref,
                 w_out_ref, w_ff1_ref, w_ff2_ref, w_ple_ref, w_gate_ref,
                 y_ref, knew_ref, vnew_ref, unew_ref,
                 kext, vt, uext, xbuf, mixbuf, x1_s, ff_s, wq_s, s_scr, p_scr):
    t = lax.rem(jnp.minimum(s, n_total - 1), n_tiles)

    @pl.when(s == 0)
    def _():
        xbuf[...] = jnp.zeros_like(xbuf)
        mixbuf[...] = jnp.zeros_like(mixbuf)
        x1_s[...] = jnp.zeros_like(x1_s)
        ff_s[...] = jnp.zeros_like(ff_s)

    @pl.when(t == 0)
    def _():
        lane = lax.broadcasted_iota(jnp.int32, (LK, 2 * HEAD_DIM), 1)
        for h in range(N_HEADS):
            kext[h] = jnp.where(lane == _mask_lane(h), NEG_INF, 0.0).astype(BF16)
        vt[...] = jnp.zeros_like(vt)
        uext[0:HIST_ROWS, :] = jnp.zeros((HIST_ROWS, D_POOL), F32)

    mix = _dot(mixbuf[...], w_out_ref[...])

    y_ref[0] = _tail(x1_s[...], ff_s[...], p_ref[0], gains_ref, w_ple_ref, w_gate_ref)

    x_prev, hn = _mix_out(xbuf[...], mix, gains_ref)
    x1_s[...] = x_prev

    def ffn_up(c):
        return jnp.square(jnp.maximum(_dot(hn, w_ff1_ref[:, c * FF_CHUNK:(c + 1) * FF_CHUNK]), 0.0)).astype(BF16)

    def ffn_down(c):
        return _dot(acts[c], w_ff2_ref[c * FF_CHUNK:(c + 1) * FF_CHUNK, :])

    def ffn_pieces(*ops):
        def run():
            total = None
            for kind, c in ops:
                if kind == "u":
                    acts[c] = ffn_up(c)
                else:
                    part = ffn_down(c)
                    total = part if total is None else total + part
            return total
        return run

    q, k, v, u = _project(x_ref[0], gains_ref, w_in_ref)
    acts = {0: ffn_up(0), 1: ffn_up(1)}
    knew_ref[0] = k
    vnew_ref[0] = v
    unew_ref[0] = u[TQ - HIST_ROWS:, :]
    uext[HIST_ROWS:HIST_ROWS + TQ, :] = u
    q_t = q.T
    v_t = v.T.astype(BF16)
    row = lax.broadcasted_iota(jnp.int32, (2 * HEAD_DIM, TQ), 0)
    lane = lax.broadcasted_iota(jnp.int32, (TQ, 2 * HEAD_DIM), 1)
    ones_rows = jnp.where(lax.broadcasted_iota(jnp.int32, (V_ROWS - HEAD_DIM, TQ), 0) == 0, 1.0, 0.0).astype(BF16)
    for h in range(N_HEADS):
        lanes = slice(2 * HEAD_DIM * (h // 2), 2 * HEAD_DIM * (h // 2 + 1))
        own = (lane < HEAD_DIM) if h % 2 == 0 else (lane >= HEAD_DIM)
        kext[h, 0:LEFT, :] = kext[h, TQ:LK, :]
        kext[h, LEFT:LK, :] = jnp.where(own, k[:, lanes], 0.0).astype(BF16)
        wq_s[h] = jnp.where(row == _mask_lane(h), 1.0, q_t[lanes, :]).astype(BF16)
        vt[h, :, 0:LEFT] = vt[h, :, TQ:LK]
        vt[h, :, LEFT:LK] = jnp.concatenate([v_t[HEAD_DIM * h:HEAD_DIM * (h + 1), :], ones_rows], axis=0)

    def scores(h):
        half = LK // 2
        s_scr[h % 2, 0:half, :] = _dot(kext[h, 0:half, :], wq_s[h])
        s_scr[h % 2, half:LK, :] = _dot(kext[h, half:LK, :], wq_s[h])

    def values(h):
        o = _dot(vt[h], p_scr[h % 2])
        return o[0:HEAD_DIM, :] / o[HEAD_DIM:HEAD_DIM + 1, :]

    fillers = [ffn_pieces(("d", 0), ("u", 2)), ffn_pieces(("d", 1), ("u", 3)), ffn_pieces(("d", 2), ("u", 4)),
               ffn_pieces(("d", 3), ("u", 5)), ffn_pieces(("d", 4), ("u", 6)), ffn_pieces(("d", 5), ("u", 7)),
               ffn_pieces(("d", 6), ("d", 7))]
    scores(0)
    scores(1)
    ff = None
    heads = []
    for h in range(N_HEADS):
        sc = s_scr[h % 2] + table_ref[h]
        p_scr[h % 2] = jnp.exp2(sc - jnp.max(sc, axis=0, keepdims=True)).astype(BF16)
        if h < len(fillers):
            part = fillers[h]()
            ff = part if ff is None else ff + part
        heads.append(values(h))
        if h + 2 < N_HEADS:
            scores(h + 2)
        if h == POOL_ROUND:
            pool = _pool(uext, TQ, t * TQ, w_pool_ref, pscale_ref[0:1, :])
            uext[0:HIST_ROWS, :] = uext[TQ:TQ + HIST_ROWS, :]
            mixbuf[:, D_ATTN:] = pool.astype(BF16)
    ff_s[...] = ff
    mixbuf[:, 0:D_ATTN] = jnp.concatenate(heads, axis=0).T.astype(BF16)
    xbuf[...] = x_ref[0]


def _prompt_drain(has_prev, p_prev_ref, p_last_ref, gains_ref, w_out_ref, w_ff1_ref, w_ff2_ref, w_ple_ref,
                  w_gate_ref, y_prev_ref, y_last_ref, xbuf, mixbuf, x1_s, ff_s):
    mix = _dot(mixbuf[...], w_out_ref[...])
    if has_prev:
        y_prev_ref[0] = _tail(x1_s[...], ff_s[...], p_prev_ref[0], gains_ref, w_ple_ref, w_gate_ref)
    x_last, hn = _mix_out(xbuf[...], mix, gains_ref)
    ff = None
    for c in range(D_FF // FF_CHUNK):
        part = _ffn_chunk(c, hn, w_ff1_ref, w_ff2_ref)
        ff = part if ff is None else ff + part
    y_last_ref[0] = _tail(x_last, ff, p_last_ref[0], gains_ref, w_ple_ref, w_gate_ref)


def _prompt_kernel(n_tiles, n_total, x_hbm, p_hbm, *refs):
    n_const = 10
    consts, (y_hbm, k_hbm, v_hbm, u_hbm) = refs[:n_const], refs[n_const:n_const + 4]
    *tile_scratch, xin, pin, yout, kbuf, vbuf, ubuf, sem_x, sem_p, sem_y, sem_o = refs[n_const + 4:]
    last = n_total - 1

    def x_copy(tile, slot):
        return pltpu.make_async_copy(x_hbm.at[tile], xin.at[slot], sem_x.at[slot])

    def p_copy(tile, slot):
        return pltpu.make_async_copy(p_hbm.at[tile], pin.at[slot], sem_p.at[slot])

    def y_copy(tile, slot):
        return pltpu.make_async_copy(yout.at[slot], y_hbm.at[tile], sem_y.at[slot])

    x_copy(0, 0).start()
    pin[...] = jnp.zeros_like(pin)

    def tile_step(s, carry):
        slot = lax.rem(s, 2)
        other = 1 - slot

        x_copy(s, slot).wait()

        @pl.when(s >= 2)
        def _():
            p_copy(s - 2, slot).wait()

        @pl.when(s >= 4)
        def _():
            y_copy(s - 4, slot).wait()

        @pl.when(s + 1 <= last)
        def _():
            x_copy(s + 1, other).start()

        @pl.when(s >= 1)
        def _():
            p_copy(s - 1, other).start()

        _prompt_tile(n_tiles, n_total, s, xin.at[pl.ds(slot, 1)], pin.at[pl.ds(slot, 1)], *consts,
                     yout.at[pl.ds(slot, 1)], kbuf, vbuf, ubuf, *tile_scratch)

        @pl.when(s >= 2)
        def _():
            y_copy(s - 2, slot).start()

        t = lax.rem(s, n_tiles)
        seq = s // n_tiles

        @pl.when(t >= n_tiles - LEFT // TQ)
        def _():
            rows = pl.ds(pl.multiple_of((t - (n_tiles - LEFT // TQ)) * TQ, TQ), TQ)
            k_out = pltpu.make_async_copy(kbuf.at[0], k_hbm.at[seq, rows], sem_o.at[0])
            v_out = pltpu.make_async_copy(vbuf.at[0], v_hbm.at[seq, rows], sem_o.at[1])
            k_out.start()
            v_out.start()
            k_out.wait()
            v_out.wait()

        @pl.when(t == n_tiles - 1)
        def _():
            u_out = pltpu.make_async_copy(ubuf.at[0], u_hbm.at[seq], sem_o.at[2])
            u_out.start()
            u_out.wait()

        return carry

    lax.fori_loop(0, n_total, tile_step, 0)

    slot_a, slot_b = n_total % 2, (n_total + 1) % 2
    p_copy(last, slot_b).start()
    if last >= 1:
        p_copy(last - 1, slot_a).wait()
    if last >= 3:
        y_copy(last - 3, slot_a).wait()
    if last >= 2:
        y_copy(last - 2, slot_b).wait()
    p_copy(last, slot_b).wait()
    table_ref, gains_ref, pscale_ref, w_in_ref, w_pool_ref, w_out_ref, w_ff1_ref, w_ff2_ref, w_ple_ref, w_gate_ref = consts
    kext, vt, uext, xbuf, mixbuf, x1_s, ff_s, wq_s, s_scr, p_scr = tile_scratch
    _prompt_drain(last >= 1, pin.at[pl.ds(slot_a, 1)], pin.at[pl.ds(slot_b, 1)], gains_ref, w_out_ref, w_ff1_ref,
                  w_ff2_ref, w_ple_ref, w_gate_ref, yout.at[pl.ds(slot_a, 1)], yout.at[pl.ds(slot_b, 1)],
                  xbuf, mixbuf, x1_s, ff_s)
    if last >= 1:
        y_copy(last - 1, slot_a).start()
    y_copy(last, slot_b).start()
    if last >= 1:
        y_copy(last - 1, slot_a).wait()
    y_copy(last, slot_b).wait()


def _resident(a):
    return pl.BlockSpec(a.shape, lambda *_: (0,) * a.ndim, pipeline_mode=pl.Buffered(1))


def _prompt_layer(x, p, table, gains, pscale, weights):
    batch, seq, _ = x.shape
    assert seq % TQ == 0 and seq >= LEFT and LEFT % TQ == 0 and D_FF // FF_CHUNK == N_HEADS
    n_tiles = seq // TQ
    n_total = batch * n_tiles
    consts = (table, gains, pscale) + tuple(weights)
    in_hbm = pl.BlockSpec(memory_space=pl.ANY)
    in_vmem = pl.BlockSpec(memory_space=pltpu.VMEM)
    y, k_new, v_new, u_new = pl.pallas_call(
        functools.partial(_prompt_kernel, n_tiles, n_total),
        in_specs=[in_hbm, in_hbm] + [in_vmem] * len(consts),
        out_specs=[in_hbm] * 4,
        out_shape=[jax.ShapeDtypeStruct((n_total, TQ, D_MODEL), F32),
                   jax.ShapeDtypeStruct((batch, LEFT, D_ATTN), F32),
                   jax.ShapeDtypeStruct((batch, LEFT, D_ATTN), F32),
                   jax.ShapeDtypeStruct((batch, HIST_ROWS, D_POOL), F32)],
        scratch_shapes=[pltpu.VMEM((N_HEADS, LK, 2 * HEAD_DIM), BF16), pltpu.VMEM((N_HEADS, V_ROWS, LK), BF16),
                        pltpu.VMEM((HIST_ROWS + TQ, D_POOL), F32), pltpu.VMEM((TQ, D_MODEL), F32),
                        pltpu.VMEM((TQ, D_ATTN + D_POOL), BF16), pltpu.VMEM((TQ, D_MODEL), F32),
                        pltpu.VMEM((TQ, D_MODEL), F32), pltpu.VMEM((N_HEADS, 2 * HEAD_DIM, TQ), BF16),
                        pltpu.VMEM((2, LK, TQ), F32), pltpu.VMEM((2, LK, TQ), BF16),
                        pltpu.VMEM((2, TQ, D_MODEL), F32), pltpu.VMEM((2, TQ, D_PLE), F32),
                        pltpu.VMEM((2, TQ, D_MODEL), F32), pltpu.VMEM((1, TQ, D_ATTN), F32),
                        pltpu.VMEM((1, TQ, D_ATTN), F32), pltpu.VMEM((1, HIST_ROWS, D_POOL), F32),
                        pltpu.SemaphoreType.DMA((2,)), pltpu.SemaphoreType.DMA((2,)),
                        pltpu.SemaphoreType.DMA((2,)), pltpu.SemaphoreType.DMA((3,))],
        compiler_params=pltpu.CompilerParams(vmem_limit_bytes=VMEM_LIMIT_BYTES),
        name="prompt_layer",
    )(x.reshape(n_total, TQ, D_MODEL), p.reshape(n_total, TQ, D_PLE), *consts)
    return y.reshape(batch, seq, D_MODEL), k_new, v_new, u_new


def _sample_kernel(n_seq, x_ref, p_ref, ck_ref, cv_ref, sp_ref, table_ref, gains_ref, pscale_ref, w_in_ref,
                   w_pool_ref, w_out_ref, w_ff1_ref, w_ff2_ref, w_ple_ref, w_gate_ref,
                   y_ref, knew_ref, vnew_ref, unew_ref, q_s, mix_s, kext, vt, uext):
    b = pl.program_id(0)
    row0 = pl.multiple_of(b * n_seq, n_seq)

    @pl.when(b == 0)
    def _():
        q, k, v, u = _project(x_ref[...], gains_ref, w_in_ref)
        q_s[...] = q
        knew_ref[...] = k
        vnew_ref[...] = v
        mix_s[:, D_ATTN:] = u
        kext[LEFT:SAMPLE_LK, :] = jnp.zeros((SAMPLE_TQ, D_ATTN), BF16)
        vt[:, LEFT:SAMPLE_LK] = jnp.zeros((D_ATTN, SAMPLE_TQ), BF16)

    pad = jnp.zeros((SAMPLE_TQ - n_seq, D_ATTN), F32)
    q = jnp.concatenate([q_s[pl.ds(row0, n_seq), :], pad], axis=0)
    v = jnp.concatenate([vnew_ref[pl.ds(row0, n_seq), :], pad], axis=0)
    u = mix_s[pl.ds(row0, n_seq), D_ATTN:]
    kext[0:LEFT, :] = ck_ref[0].astype(BF16)
    kext[LEFT:LEFT + n_seq, :] = knew_ref[pl.ds(row0, n_seq), :].astype(BF16)
    vt[:, 0:LEFT] = cv_ref[0].T.astype(BF16)
    vt[:, LEFT:SAMPLE_LK] = v.T.astype(BF16)
    uext[0:HIST_ROWS, :] = sp_ref[0]
    uext[HIST_ROWS:HIST_ROWS + n_seq, :] = u
    unew_ref[0] = u[n_seq - HIST_ROWS:, :]

    kk = lax.broadcasted_iota(jnp.int32, (SAMPLE_LK, SAMPLE_TQ), 0)
    attn = _attention_t(q.T, kext, vt, table_ref, kk < LEFT + n_seq).T
    pool = _pool(uext, n_seq, POOL_HIST, w_pool_ref, pscale_ref[0:1, :])
    mix_s[pl.ds(row0, n_seq), 0:D_ATTN] = attn[0:n_seq, :]
    mix_s[pl.ds(row0, n_seq), D_ATTN:] = pool

    @pl.when(b == pl.num_programs(0) - 1)
    def _():
        y_ref[...] = _finish(x_ref[...], mix_s[...].astype(BF16), p_ref[...], gains_ref, w_out_ref, w_ff1_ref,
                             w_ff2_ref, w_ple_ref, w_gate_ref)


def _sample_layer(x, p, cache_k, cache_v, state_pool, table, gains, pscale, weights):
    batch, n_seq, _ = x.shape
    rows = batch * n_seq
    assert cache_k.shape[1] == LEFT and HIST_ROWS <= n_seq <= SAMPLE_TQ and n_seq % 8 == 0
    full = lambda shape: pl.BlockSpec(shape, lambda b: (0,) * len(shape))
    per_b = lambda shape: pl.BlockSpec((1,) + shape, lambda b: (b, 0, 0))
    return pl.pallas_call(
        functools.partial(_sample_kernel, n_seq),
        grid=(batch,),
        in_specs=[full((rows, D_MODEL)), full((rows, D_PLE)), per_b((LEFT, D_ATTN)), per_b((LEFT, D_ATTN)),
                  per_b((HIST_ROWS, D_POOL)), full((N_HEADS, SAMPLE_LK, SAMPLE_TQ))]
        + [_resident(a) for a in (gains, pscale) + tuple(weights)],
        out_specs=[full((rows, D_MODEL)), full((rows, D_ATTN)), full((rows, D_ATTN)), per_b((HIST_ROWS, D_POOL))],
        out_shape=[jax.ShapeDtypeStruct((rows, D_MODEL), F32), jax.ShapeDtypeStruct((rows, D_ATTN), F32),
                   jax.ShapeDtypeStruct((rows, D_ATTN), F32),
                   jax.ShapeDtypeStruct((batch, HIST_ROWS, D_POOL), F32)],
        scratch_shapes=[pltpu.VMEM((rows, D_ATTN), F32), pltpu.VMEM((rows, D_ATTN + D_POOL), F32),
                        pltpu.VMEM((SAMPLE_LK, D_ATTN), BF16), pltpu.VMEM((D_ATTN, SAMPLE_LK), BF16),
                        pltpu.VMEM((HIST_ROWS + n_seq, D_POOL), F32)],
        compiler_params=pltpu.CompilerParams(dimension_semantics=("arbitrary",),
                                             vmem_limit_bytes=VMEM_LIMIT_BYTES),
        name="sample_layer",
    )(x.reshape(rows, D_MODEL), p.reshape(rows, D_PLE), cache_k, cache_v, state_pool, table, gains, pscale,
      *weights)


def kernel(x_prompt, x_sample, cache_k, cache_v, state_pool, p_prompt, p_sample, g_mix_pre, g_mix_post,
           g_ff_pre, g_ff_post, g_ple_post, w_in, rel_bias, w_pool, pool_scale, w_out, w_ff1, w_ff2, w_ple,
           w_ple_gate):
    depth = w_in.shape[0]
    batch, seq, _ = x_prompt.shape
    dec_batch, dec_seq, _ = x_sample.shape
    xp, xs = x_prompt, x_sample
    outs = [[] for _ in range(6)]
    for i in range(depth):
        table = _bias_table(rel_bias[i])
        gains = jnp.concatenate(
            [g_mix_pre[i:i + 1], g_mix_post[i:i + 1], g_ff_pre[i:i + 1], g_ff_post[i:i + 1], g_ple_post[i:i + 1],
             jnp.zeros((3, D_MODEL), F32)], axis=0)
        pscale = jnp.broadcast_to(pool_scale[i:i + 1], (PSCALE_ROWS, D_POOL))
        weights = tuple(w[i].astype(BF16) for w in (w_in, w_pool, w_out, w_ff1, w_ff2, w_ple, w_ple_gate))
        xp, kp, vp, up = _prompt_layer(xp, p_prompt[i], table, gains, pscale, weights)
        sp = jnp.pad(state_pool[i], ((0, 0), (HIST_ROWS - POOL_HIST, 0), (0, 0)))
        xs2, kn, vn, un = _sample_layer(xs, p_sample[i], cache_k[i].reshape(dec_batch, LEFT, D_ATTN),
                                        cache_v[i].reshape(dec_batch, LEFT, D_ATTN), sp, table, gains, pscale,
                                        weights)
        xs = xs2.reshape(dec_batch, dec_seq, D_MODEL)
        outs[0].append(kp.reshape(batch, LEFT, N_HEADS, HEAD_DIM))
        outs[1].append(vp.reshape(batch, LEFT, N_HEADS, HEAD_DIM))
        outs[2].append(up[:, HIST_ROWS - POOL_HIST:, :])
        outs[3].append(kn.reshape(dec_batch, dec_seq, N_HEADS, HEAD_DIM))
        outs[4].append(vn.reshape(dec_batch, dec_seq, N_HEADS, HEAD_DIM))
        outs[5].append(un[:, HIST_ROWS - POOL_HIST:, :])
    return (xp, xs) + tuple(jnp.stack(o) for o in outs)
```

```python
import functools

import jax
import jax.numpy as jnp
from jax import lax
from jax.experimental import pallas as pl
from jax.experimental.pallas import tpu as pltpu

D_MODEL = 1024
D_ATTN = 512
D_POOL = 512
HEAD_DIM = 64
N_HEADS = 8
CHUNK = 64
LEFT_CHUNKS = 8
LEFT = LEFT_CHUNKS * CHUNK
REL_CLIP = 128
N_REL = 2 * REL_CLIP + 1
POOL_WINDOWS = (2, 4, 8, 16)
POOL_GROUP_DIM = 128
POOL_HIST = 15
HIST_ROWS = 16
D_FF = 4096
FF_CHUNK = 512
D_PLE = 256
EPS = 1e-6
NEG_INF = -1e30
ATTN_SCALE = HEAD_DIM ** -0.5
LOG2E = 1.4426950408889634
V_ROWS = HEAD_DIM + 16

TQ = 256
LK = LEFT + TQ
SAMPLE_TQ = 128
SAMPLE_LK = LEFT + SAMPLE_TQ
TABLE_PAD = 256
V7X_VMEM_BYTES = 64 * 1024 * 1024
VMEM_LIMIT_BYTES = V7X_VMEM_BYTES * 7 // 8
POOL_ROUND = 2
PSCALE_ROWS = 8

F32 = jnp.float32
BF16 = jnp.bfloat16


def _bias_table_kernel(rbt_ref, out_ref, buf_a, buf_b):
    far = jnp.broadcast_to(rbt_ref[0, 0:1, :], (TABLE_PAD + LEFT - REL_CLIP, TQ))
    buf_a[0:TABLE_PAD + LEFT - REL_CLIP, :] = far
    buf_b[0:TABLE_PAD, :] = far[0:TABLE_PAD]
    buf_a[TABLE_PAD + LEFT - REL_CLIP:TABLE_PAD + LEFT + REL_CLIP, :] = jnp.broadcast_to(
        rbt_ref[0, 0:2 * REL_CLIP, :], (2 * REL_CLIP, TQ))
    buf_a[TABLE_PAD + LEFT + REL_CLIP:TABLE_PAD + LK, :] = jnp.broadcast_to(
        rbt_ref[0, 2 * REL_CLIP:2 * REL_CLIP + 1, :], (LK - LEFT - REL_CLIP, TQ))

    rows = 128
    qidx = lax.broadcasted_iota(jnp.int32, (rows, TQ), 1)
    src, dst = buf_a, buf_b
    for b in range(TQ.bit_length() - 1):
        s = 1 << b
        bit = (qidx & s) != 0
        for r0 in range(TABLE_PAD, TABLE_PAD + LK, rows):
            dst[r0:r0 + rows, :] = jnp.where(bit, src[r0 - s:r0 - s + rows, :], src[r0:r0 + rows, :])
        src, dst = dst, src

    kk = lax.broadcasted_iota(jnp.int32, (LK, TQ), 0)
    qq = lax.broadcasted_iota(jnp.int32, (LK, TQ), 1)
    d = (kk >> 6) - (qq >> 6)
    out_ref[0] = jnp.where((d >= 0) & (d <= LEFT_CHUNKS), src[TABLE_PAD:TABLE_PAD + LK, :] * LOG2E, NEG_INF)


def _bias_table(rel_bias):
    rbt = jnp.pad(rel_bias[:, ::-1], ((0, 0), (0, 384 - N_REL)))[:, :, None]
    return pl.pallas_call(
        _bias_table_kernel,
        grid=(N_HEADS,),
        in_specs=[pl.BlockSpec((1, 384, 1), lambda h: (h, 0, 0))],
        out_specs=pl.BlockSpec((1, LK, TQ), lambda h: (h, 0, 0)),
        out_shape=jax.ShapeDtypeStruct((N_HEADS, LK, TQ), F32),
        scratch_shapes=[pltpu.VMEM((TABLE_PAD + LK, TQ), F32), pltpu.VMEM((TABLE_PAD + LK, TQ), F32)],
        name="bias_table",
    )(rbt)


def _rmsnorm(x, g):
    y = x * lax.rsqrt(jnp.mean(x * x, axis=-1, keepdims=True) + EPS)
    return y * g


def _dot(a, b):
    return jnp.dot(a, b, preferred_element_type=F32)


def _project(x, gains_ref, w_in_ref):
    h = _rmsnorm(x, gains_ref[0:1, :]).astype(BF16)
    qkvu = _dot(h, w_in_ref[...])
    q = qkvu[:, 0:D_ATTN] * (ATTN_SCALE * LOG2E)
    k = qkvu[:, D_ATTN:2 * D_ATTN]
    v = qkvu[:, 2 * D_ATTN:3 * D_ATTN]
    u = qkvu[:, 3 * D_ATTN:]
    return q, k, v, u


def _attention_t(q_t, kext_ref, vt_ref, table_ref, valid):
    nq = q_t.shape[1]
    row = lax.broadcasted_iota(jnp.int32, (2 * HEAD_DIM, nq), 0)
    scores = []
    for h in range(N_HEADS):
        pair = slice(2 * HEAD_DIM * (h // 2), 2 * HEAD_DIM * (h // 2 + 1))
        own = (row < HEAD_DIM) if h % 2 == 0 else (row >= HEAD_DIM)
        scores.append(_dot(kext_ref[:, pair], jnp.where(own, q_t[pair, :], 0.0).astype(BF16)))
    probs, denoms = [], []
    for h in range(N_HEADS):
        s = jnp.where(valid, scores[h] + table_ref[h], NEG_INF)
        p = jnp.exp2(s - jnp.max(s, axis=0, keepdims=True))
        denoms.append(jnp.sum(p, axis=0, keepdims=True))
        probs.append(p.astype(BF16))
    outs = [_dot(vt_ref[HEAD_DIM * h:HEAD_DIM * (h + 1), :], probs[h]) / denoms[h] for h in range(N_HEADS)]
    return jnp.concatenate(outs, axis=0)


def _pool(uext_ref, n_rows, frames_before, w_pool_ref, pool_scale):
    rows = lax.broadcasted_iota(jnp.int32, (n_rows, POOL_GROUP_DIM), 0)
    outs = []
    for g, w in enumerate(POOL_WINDOWS):
        lo = g * POOL_GROUP_DIM
        acc = uext_ref[0:HIST_ROWS + n_rows, lo:lo + POOL_GROUP_DIM]
        cur = acc[HIST_ROWS:, :]
        span = 1
        while span < w:
            acc = acc + pltpu.roll(acc, span, 0)
            span *= 2
        cnt = jnp.minimum(w, frames_before + rows + 1).astype(F32)
        diff = acc[HIST_ROWS:, :] / cnt - cur
        outs.append(_dot(diff.astype(BF16), w_pool_ref[g]))
    return jnp.concatenate(outs, axis=-1) * pool_scale


def _mix_out(x, mix, gains_ref):
    x = x + _rmsnorm(mix, gains_ref[1:2, :])
    return x, _rmsnorm(x, gains_ref[2:3, :]).astype(BF16)


def _ffn_chunk(c, hn, w_ff1_ref, w_ff2_ref):
    chunk = slice(c * FF_CHUNK, (c + 1) * FF_CHUNK)
    act = jnp.square(jnp.maximum(_dot(hn, w_ff1_ref[:, chunk]), 0.0)).astype(BF16)
    return _dot(act, w_ff2_ref[chunk, :])


def _tail(x, ff, p, gains_ref, w_ple_ref, w_gate_ref):
    x = x + _rmsnorm(ff, gains_ref[3:4, :])
    gate = jax.nn.sigmoid(_dot(x.astype(BF16), w_gate_ref[...]))
    ple = gate * _dot(p.astype(BF16), w_ple_ref[...])
    return x + _rmsnorm(ple, gains_ref[4:5, :])


def _finish(x, mixed, p, gains_ref, w_out_ref, w_ff1_ref, w_ff2_ref, w_ple_ref, w_gate_ref):
    x, hn = _mix_out(x, _dot(mixed, w_out_ref[...]), gains_ref)
    ff = None
    for c in range(D_FF // FF_CHUNK):
        part = _ffn_chunk(c, hn, w_ff1_ref, w_ff2_ref)
        ff = part if ff is None else ff + part
    return _tail(x, ff, p, gains_ref, w_ple_ref, w_gate_ref)


def _mask_lane(h):
    return HEAD_DIM if h % 2 == 0 else 0


def _prompt_tile(n_tiles, n_total, s, x_ref, p_ref, table_ref, gains_ref, pscale_ref, w_in_ref, w_pool_ref,
                 w_out_ref, w_ff1_ref, w_ff2_ref, w_ple_ref, w_gate_ref,
                 y_ref, knew_ref, vnew_ref, unew_ref,
                 kext, vt, uext, xbuf, mixbuf, x1_s, ff_s, wq_s, s_scr, p_scr):
    t = lax.rem(jnp.minimum(s, n_total - 1), n_tiles)

    @pl.when(s == 0)
    def _():
        xbuf[...] = jnp.zeros_like(xbuf)
        mixbuf[...] = jnp.zeros_like(mixbuf)
        x1_s[...] = jnp.zeros_like(x1_s)
        ff_s[...] = jnp.zeros_like(ff_s)

    @pl.when(t == 0)
    def _():
        lane = lax.broadcasted_iota(jnp.int32, (LK, 2 * HEAD_DIM), 1)
        for h in range(N_HEADS):
            kext[h] = jnp.where(lane == _mask_lane(h), NEG_INF, 0.0).astype(BF16)
        vt[...] = jnp.zeros_like(vt)
        uext[0:HIST_ROWS, :] = jnp.zeros((HIST_ROWS, D_POOL), F32)

    mix = _dot(mixbuf[...], w_out_ref[...])

    y_ref[0] = _tail(x1_s[...], ff_s[...], p_ref[0], gains_ref, w_ple_ref, w_gate_ref)

    x_prev, hn = _mix_out(xbuf[...], mix, gains_ref)
    x1_s[...] = x_prev

    def ffn_up(c):
        return jnp.square(jnp.maximum(_dot(hn, w_ff1_ref[:, c * FF_CHUNK:(c + 1) * FF_CHUNK]), 0.0)).astype(BF16)

    def ffn_down(c):
        return _dot(acts[c], w_ff2_ref[c * FF_CHUNK:(c + 1) * FF_CHUNK, :])

    def ffn_pieces(*ops):
        def run():
            total = None
            for kind, c in ops:
                if kind == "u":
                    acts[c] = ffn_up(c)
                else:
                    part = ffn_down(c)
                    total = part if total is None else total + part
            return total
        return run

    q, k, v, u = _project(x_ref[0], gains_ref, w_in_ref)
    acts = {0: ffn_up(0), 1: ffn_up(1)}
    knew_ref[0] = k
    vnew_ref[0] = v
    unew_ref[0] = u[TQ - HIST_ROWS:, :]
    uext[HIST_ROWS:HIST_ROWS + TQ, :] = u
    q_t = q.T
    v_t = v.T.astype(BF16)
    row = lax.broadcasted_iota(jnp.int32, (2 * HEAD_DIM, TQ), 0)
    lane = lax.broadcasted_iota(jnp.int32, (TQ, 2 * HEAD_DIM), 1)
    ones_rows = jnp.where(lax.broadcasted_iota(jnp.int32, (V_ROWS - HEAD_DIM, TQ), 0) == 0, 1.0, 0.0).astype(BF16)
    for h in range(N_HEADS):
        lanes = slice(2 * HEAD_DIM * (h // 2), 2 * HEAD_DIM * (h // 2 + 1))
        own = (lane < HEAD_DIM) if h % 2 == 0 else (lane >= HEAD_DIM)
        kext[h, 0:LEFT, :] = kext[h, TQ:LK, :]
        kext[h, LEFT:LK, :] = jnp.where(own, k[:, lanes], 0.0).astype(BF16)
        wq_s[h] = jnp.where(row == _mask_lane(h), 1.0, q_t[lanes, :]).astype(BF16)
        vt[h, :, 0:LEFT] = vt[h, :, TQ:LK]
        vt[h, :, LEFT:LK] = jnp.concatenate([v_t[HEAD_DIM * h:HEAD_DIM * (h + 1), :], ones_rows], axis=0)

    def scores(h):
        half = LK // 2
        s_scr[h % 2, 0:half, :] = _dot(kext[h, 0:half, :], wq_s[h])
        s_scr[h % 2, half:LK, :] = _dot(kext[h, half:LK, :], wq_s[h])

    def values(h):
        o = _dot(vt[h], p_scr[h % 2])
        return o[0:HEAD_DIM, :] / o[HEAD_DIM:HEAD_DIM + 1, :]

    fillers = [ffn_pieces(("d", 0), ("u", 2)), ffn_pieces(("d", 1), ("u", 3)), ffn_pieces(("d", 2), ("u", 4)),
               ffn_pieces(("d", 3), ("u", 5)), ffn_pieces(("d", 4), ("u", 6)), ffn_pieces(("d", 5), ("u", 7)),
               ffn_pieces(("d", 6), ("d", 7))]
    scores(0)
    scores(1)
    ff = None
    heads = []
    for h in range(N_HEADS):
        sc = s_scr[h % 2] + table_ref[h]
        p_scr[h % 2] = jnp.exp2(sc - jnp.max(sc, axis=0, keepdims=True)).astype(BF16)
        if h < len(fillers):
            part = fillers[h]()
            ff = part if ff is None else ff + part
        heads.append(values(h))
        if h + 2 < N_HEADS:
            scores(h + 2)
        if h == POOL_ROUND:
            pool = _pool(uext, TQ, t * TQ, w_pool_ref, pscale_ref[0:1, :])
            uext[0:HIST_ROWS, :] = uext[TQ:TQ + HIST_ROWS, :]
            mixbuf[:, D_ATTN:] = pool.astype(BF16)
    ff_s[...] = ff
    mixbuf[:, 0:D_ATTN] = jnp.concatenate(heads, axis=0).T.astype(BF16)
    xbuf[...] = x_ref[0]


def _prompt_drain(has_prev, p_prev_ref, p_last_ref, gains_ref, w_out_ref, w_ff1_ref, w_ff2_ref, w_ple_ref,
                  w_gate_ref, y_prev_ref, y_last_ref, xbuf, mixbuf, x1_s, ff_s):
    mix = _dot(mixbuf[...], w_out_ref[...])
    if has_prev:
        y_prev_ref[0] = _tail(x1_s[...], ff_s[...], p_prev_ref[0], gains_ref, w_ple_ref, w_gate_ref)
    x_last, hn = _mix_out(xbuf[...], mix, gains_ref)
    ff = None
    for c in range(D_FF // FF_CHUNK):
        part = _ffn_chunk(c, hn, w_ff1_ref, w_ff2_ref)
        ff = part if ff is None else ff + part
    y_last_ref[0] = _tail(x_last, ff, p_last_ref[0], gains_ref, w_ple_ref, w_gate_ref)


def _prompt_kernel(n_tiles, n_total, x_hbm, p_hbm, *refs):
    n_const = 10
    consts, (y_hbm, k_hbm, v_hbm, u_hbm) = refs[:n_const], refs[n_const:n_const + 4]
    *tile_scratch, xin, pin, yout, kbuf, vbuf, ubuf, sem_x, sem_p, sem_y, sem_o = refs[n_const + 4:]
    last = n_total - 1

    def x_copy(tile, slot):
        return pltpu.make_async_copy(x_hbm.at[tile], xin.at[slot], sem_x.at[slot])

    def p_copy(tile, slot):
        return pltpu.make_async_copy(p_hbm.at[tile], pin.at[slot], sem_p.at[slot])

    def y_copy(tile, slot):
        return pltpu.make_async_copy(yout.at[slot], y_hbm.at[tile], sem_y.at[slot])

    def u_copy(seq, slot):
        return pltpu.make_async_copy(ubuf.at[slot], u_hbm.at[seq], sem_o.at[2])

    x_copy(0, 0).start()
    pin[...] = jnp.zeros_like(pin)

    def tile_step(s, carry):
        slot = lax.rem(s, 2)
        other = 1 - slot

        x_copy(s, slot).wait()

        @pl.when(s >= 2)
        def _():
            p_copy(s - 2, slot).wait()

        @pl.when(s >= 4)
        def _():
            y_copy(s - 4, slot).wait()

        @pl.when(s + 1 <= last)
        def _():
            x_copy(s + 1, other).start()

        @pl.when(s >= 1)
        def _():
            p_copy(s - 1, other).start()

        _prompt_tile(n_tiles, n_total, s, xin.at[pl.ds(slot, 1)], pin.at[pl.ds(slot, 1)], *consts,
                     yout.at[pl.ds(slot, 1)], kbuf, vbuf, ubuf.at[pl.ds(slot, 1)], *tile_scratch)

        @pl.when(s >= 2)
        def _():
            y_copy(s - 2, slot).start()

        t = lax.rem(s, n_tiles)
        seq = s // n_tiles

        @pl.when(t >= n_tiles - LEFT // TQ)
        def _():
            rows = pl.ds(pl.multiple_of((t - (n_tiles - LEFT // TQ)) * TQ, TQ), TQ)
            k_out = pltpu.make_async_copy(kbuf.at[0], k_hbm.at[seq, rows], sem_o.at[0])
            v_out = pltpu.make_async_copy(vbuf.at[0], v_hbm.at[seq, rows], sem_o.at[1])
            k_out.start()
            v_out.start()
            k_out.wait()
            v_out.wait()

        @pl.when((s >= 1) & (t == 0))
        def _():
            u_copy(seq - 1, other).wait()

        @pl.when(t == n_tiles - 1)
        def _():
            u_copy(seq, slot).start()

        return carry

    lax.fori_loop(0, n_total, tile_step, 0)
    u_copy(last // n_tiles, last % 2).wait()

    slot_a, slot_b = n_total % 2, (n_total + 1) % 2
    p_copy(last, slot_b).start()
    if last >= 1:
        p_copy(last - 1, slot_a).wait()
    if last >= 3:
        y_copy(last - 3, slot_a).wait()
    if last >= 2:
        y_copy(last - 2, slot_b).wait()
    p_copy(last, slot_b).wait()
    table_ref, gains_ref, pscale_ref, w_in_ref, w_pool_ref, w_out_ref, w_ff1_ref, w_ff2_ref, w_ple_ref, w_gate_ref = consts
    kext, vt, uext, xbuf, mixbuf, x1_s, ff_s, wq_s, s_scr, p_scr = tile_scratch
    _prompt_drain(last >= 1, pin.at[pl.ds(slot_a, 1)], pin.at[pl.ds(slot_b, 1)], gains_ref, w_out_ref, w_ff1_ref,
                  w_ff2_ref, w_ple_ref, w_gate_ref, yout.at[pl.ds(slot_a, 1)], yout.at[pl.ds(slot_b, 1)],
                  xbuf, mixbuf, x1_s, ff_s)
    if last >= 1:
        y_copy(last - 1, slot_a).start()
    y_copy(last, slot_b).start()
    if last >= 1:
        y_copy(last - 1, slot_a).wait()
    y_copy(last, slot_b).wait()


def _resident(a):
    return pl.BlockSpec(a.shape, lambda *_: (0,) * a.ndim, pipeline_mode=pl.Buffered(1))


def _prompt_layer(x, p, table, gains, pscale, weights):
    batch, seq, _ = x.shape
    assert seq % TQ == 0 and seq >= LEFT and LEFT % TQ == 0 and D_FF // FF_CHUNK == N_HEADS
    n_tiles = seq // TQ
    n_total = batch * n_tiles
    consts = (table, gains, pscale) + tuple(weights)
    in_hbm = pl.BlockSpec(memory_space=pl.ANY)
    in_vmem = pl.BlockSpec(memory_space=pltpu.VMEM)
    y, k_new, v_new, u_new = pl.pallas_call(
        functools.partial(_prompt_kernel, n_tiles, n_total),
        in_specs=[in_hbm, in_hbm] + [in_vmem] * len(consts),
        out_specs=[in_hbm] * 4,
        out_shape=[jax.ShapeDtypeStruct((n_total, TQ, D_MODEL), F32),
                   jax.ShapeDtypeStruct((batch, LEFT, D_ATTN), F32),
                   jax.ShapeDtypeStruct((batch, LEFT, D_ATTN), F32),
                   jax.ShapeDtypeStruct((batch, HIST_ROWS, D_POOL), F32)],
        scratch_shapes=[pltpu.VMEM((N_HEADS, LK, 2 * HEAD_DIM), BF16), pltpu.VMEM((N_HEADS, V_ROWS, LK), BF16),
                        pltpu.VMEM((HIST_ROWS + TQ, D_POOL), F32), pltpu.VMEM((TQ, D_MODEL), F32),
                        pltpu.VMEM((TQ, D_ATTN + D_POOL), BF16), pltpu.VMEM((TQ, D_MODEL), F32),
                        pltpu.VMEM((TQ, D_MODEL), F32), pltpu.VMEM((N_HEADS, 2 * HEAD_DIM, TQ), BF16),
                        pltpu.VMEM((2, LK, TQ), F32), pltpu.VMEM((2, LK, TQ), BF16),
                        pltpu.VMEM((2, TQ, D_MODEL), F32), pltpu.VMEM((2, TQ, D_PLE), F32),
                        pltpu.VMEM((2, TQ, D_MODEL), F32), pltpu.VMEM((1, TQ, D_ATTN), F32),
                        pltpu.VMEM((1, TQ, D_ATTN), F32), pltpu.VMEM((2, HIST_ROWS, D_POOL), F32),
                        pltpu.SemaphoreType.DMA((2,)), pltpu.SemaphoreType.DMA((2,)),
                        pltpu.SemaphoreType.DMA((2,)), pltpu.SemaphoreType.DMA((3,))],
        compiler_params=pltpu.CompilerParams(vmem_limit_bytes=VMEM_LIMIT_BYTES),
        name="prompt_layer",
    )(x.reshape(n_total, TQ, D_MODEL), p.reshape(n_total, TQ, D_PLE), *consts)
    return y.reshape(batch, seq, D_MODEL), k_new, v_new, u_new


def _sample_kernel(n_seq, x_ref, p_ref, ck_ref, cv_ref, sp_ref, table_ref, gains_ref, pscale_ref, w_in_ref,
                   w_pool_ref, w_out_ref, w_ff1_ref, w_ff2_ref, w_ple_ref, w_gate_ref,
                   y_ref, knew_ref, vnew_ref, unew_ref, q_s, mix_s, kext, vt, uext):
    b = pl.program_id(0)
    row0 = pl.multiple_of(b * n_seq, n_seq)

    @pl.when(b == 0)
    def _():
        q, k, v, u = _project(x_ref[...], gains_ref, w_in_ref)
        q_s[...] = q
        knew_ref[...] = k
        vnew_ref[...] = v
        mix_s[:, D_ATTN:] = u
        kext[LEFT:SAMPLE_LK, :] = jnp.zeros((SAMPLE_TQ, D_ATTN), BF16)
        vt[:, LEFT:SAMPLE_LK] = jnp.zeros((D_ATTN, SAMPLE_TQ), BF16)

    pad = jnp.zeros((SAMPLE_TQ - n_seq, D_ATTN), F32)
    q = jnp.concatenate([q_s[pl.ds(row0, n_seq), :], pad], axis=0)
    v = jnp.concatenate([vnew_ref[pl.ds(row0, n_seq), :], pad], axis=0)
    u = mix_s[pl.ds(row0, n_seq), D_ATTN:]
    kext[0:LEFT, :] = ck_ref[0].astype(BF16)
    kext[LEFT:LEFT + n_seq, :] = knew_ref[pl.ds(row0, n_seq), :].astype(BF16)
    vt[:, 0:LEFT] = cv_ref[0].T.astype(BF16)
    vt[:, LEFT:SAMPLE_LK] = v.T.astype(BF16)
    uext[0:HIST_ROWS, :] = sp_ref[0]
    uext[HIST_ROWS:HIST_ROWS + n_seq, :] = u
    unew_ref[0] = u[n_seq - HIST_ROWS:, :]

    kk = lax.broadcasted_iota(jnp.int32, (SAMPLE_LK, SAMPLE_TQ), 0)
    attn = _attention_t(q.T, kext, vt, table_ref, kk < LEFT + n_seq).T
    pool = _pool(uext, n_seq, POOL_HIST, w_pool_ref, pscale_ref[0:1, :])
    mix_s[pl.ds(row0, n_seq), 0:D_ATTN] = attn[0:n_seq, :]
    mix_s[pl.ds(row0, n_seq), D_ATTN:] = pool

    @pl.when(b == pl.num_programs(0) - 1)
    def _():
        y_ref[...] = _finish(x_ref[...], mix_s[...].astype(BF16), p_ref[...], gains_ref, w_out_ref, w_ff1_ref,
                             w_ff2_ref, w_ple_ref, w_gate_ref)


def _sample_layer(x, p, cache_k, cache_v, state_pool, table, gains, pscale, weights):
    batch, n_seq, _ = x.shape
    rows = batch * n_seq
    assert cache_k.shape[1] == LEFT and HIST_ROWS <= n_seq <= SAMPLE_TQ and n_seq % 8 == 0
    full = lambda shape: pl.BlockSpec(shape, lambda b: (0,) * len(shape))
    per_b = lambda shape: pl.BlockSpec((1,) + shape, lambda b: (b, 0, 0))
    return pl.pallas_call(
        functools.partial(_sample_kernel, n_seq),
        grid=(batch,),
        in_specs=[full((rows, D_MODEL)), full((rows, D_PLE)), per_b((LEFT, D_ATTN)), per_b((LEFT, D_ATTN)),
                  per_b((HIST_ROWS, D_POOL)), full((N_HEADS, SAMPLE_LK, SAMPLE_TQ))]
        + [_resident(a) for a in (gains, pscale) + tuple(weights)],
        out_specs=[full((rows, D_MODEL)), full((rows, D_ATTN)), full((rows, D_ATTN)), per_b((HIST_ROWS, D_POOL))],
        out_shape=[jax.ShapeDtypeStruct((rows, D_MODEL), F32), jax.ShapeDtypeStruct((rows, D_ATTN), F32),
                   jax.ShapeDtypeStruct((rows, D_ATTN), F32),
                   jax.ShapeDtypeStruct((batch, HIST_ROWS, D_POOL), F32)],
        scratch_shapes=[pltpu.VMEM((rows, D_ATTN), F32), pltpu.VMEM((rows, D_ATTN + D_POOL), F32),
                        pltpu.VMEM((SAMPLE_LK, D_ATTN), BF16), pltpu.VMEM((D_ATTN, SAMPLE_LK), BF16),
                        pltpu.VMEM((HIST_ROWS + n_seq, D_POOL), F32)],
        compiler_params=pltpu.CompilerParams(dimension_semantics=("arbitrary",),
                                             vmem_limit_bytes=VMEM_LIMIT_BYTES),
        name="sample_layer",
    )(x.reshape(rows, D_MODEL), p.reshape(rows, D_PLE), cache_k, cache_v, state_pool, table, gains, pscale,
      *weights)


def kernel(x_prompt, x_sample, cache_k, cache_v, state_pool, p_prompt, p_sample, g_mix_pre, g_mix_post,
           g_ff_pre, g_ff_post, g_ple_post, w_in, rel_bias, w_pool, pool_scale, w_out, w_ff1, w_ff2, w_ple,
           w_ple_gate):
    depth = w_in.shape[0]
    batch, seq, _ = x_prompt.shape
    dec_batch, dec_seq, _ = x_sample.shape
    xp, xs = x_prompt, x_sample
    outs = [[] for _ in range(6)]
    for i in range(depth):
        table = _bias_table(rel_bias[i])
        gains = jnp.concatenate(
            [g_mix_pre[i:i + 1], g_mix_post[i:i + 1], g_ff_pre[i:i + 1], g_ff_post[i:i + 1], g_ple_post[i:i + 1],
             jnp.zeros((3, D_MODEL), F32)], axis=0)
        pscale = jnp.broadcast_to(pool_scale[i:i + 1], (PSCALE_ROWS, D_POOL))
        weights = tuple(w[i].astype(BF16) for w in (w_in, w_pool, w_out, w_ff1, w_ff2, w_ple, w_ple_gate))
        xp, kp, vp, up = _prompt_layer(xp, p_prompt[i], table, gains, pscale, weights)
        sp = jnp.pad(state_pool[i], ((0, 0), (HIST_ROWS - POOL_HIST, 0), (0, 0)))
        xs2, kn, vn, un = _sample_layer(xs, p_sample[i], cache_k[i].reshape(dec_batch, LEFT, D_ATTN),
                                        cache_v[i].reshape(dec_batch, LEFT, D_ATTN), sp, table, gains, pscale,
                                        weights)
        xs = xs2.reshape(dec_batch, dec_seq, D_MODEL)
        outs[0].append(kp.reshape(batch, LEFT, N_HEADS, HEAD_DIM))
        outs[1].append(vp.reshape(batch, LEFT, N_HEADS, HEAD_DIM))
        outs[2].append(up[:, HIST_ROWS - POOL_HIST:, :])
        outs[3].append(kn.reshape(dec_batch, dec_seq, N_HEADS, HEAD_DIM))
        outs[4].append(vn.reshape(dec_batch, dec_seq, N_HEADS, HEAD_DIM))
        outs[5].append(un[:, HIST_ROWS - POOL_HIST:, :])
    return (xp, xs) + tuple(jnp.stack(o) for o in outs)
```
